```python
import math
import jax, jax.numpy as jnp
from jax import lax
import numpy as np

D_MODEL = 1024
BATCH = 16
SEQ = 4096
DEPTH = 1

CHUNK = 64
LEFT_CHUNKS = 8
A_HEADS = 8
A_HEAD_DIM = 64
A_WIDTH = A_HEADS * A_HEAD_DIM
REL_CLIP = 128
B_HEADS = 4
B_HEAD_DIM = 128
B_WIDTH = B_HEADS * B_HEAD_DIM
CONV_K = 4
PEER_HEADS = 8
N_KEYS = 128
N_EXPERTS = N_KEYS * N_KEYS
PEER_TOPK = 16
D_KEY = 256
PEER_BLOCK = 128
EPS = 1e-6
IN_SIZES = [A_WIDTH, A_WIDTH, A_WIDTH, 2 * B_WIDTH, B_WIDTH, B_WIDTH, B_HEADS, B_HEADS, D_MODEL, D_MODEL]
IN_COLS = sum(IN_SIZES)

kernel_name = "hybrid_chunk_attn_mlstm_peer_block"


def rmsnorm(x, g):
    xf = x.astype(jnp.float32)
    y = xf * lax.rsqrt(jnp.mean(xf * xf, axis=-1, keepdims=True) + EPS)
    return (y * g.astype(jnp.float32)).astype(x.dtype)


def modulate(h, shift, scale):
    return h * (1.0 + scale[:, None, :]) + shift[:, None, :]


def causal_depthwise_conv(u, w, b):
    C = u.shape[-1]
    out = lax.conv_general_dilated(
        u, w[:, None, :].astype(u.dtype), window_strides=(1,), padding=[(CONV_K - 1, 0)],
        dimension_numbers=("NWC", "WIO", "NWC"), feature_group_count=C)
    return out + b


def chunk_band_attention(q, k, v, rel_bias):
    B, S, H, dh = q.shape
    nc = S // CHUNK
    pad = LEFT_CHUNKS * CHUNK
    band = pad + CHUNK
    kp = jnp.pad(k, ((0, 0), (pad, 0), (0, 0), (0, 0)))
    vp = jnp.pad(v, ((0, 0), (pad, 0), (0, 0), (0, 0)))
    qc = q.reshape(B, nc, CHUNK, H, dh).transpose(1, 0, 2, 3, 4)
    k_idx = jnp.arange(band)
    rel = jnp.clip(jnp.arange(CHUNK)[:, None] + pad - k_idx[None, :], -REL_CLIP, REL_CLIP) + REL_CLIP
    bias = rel_bias[:, rel].astype(jnp.float32)
    scale = dh ** -0.5

    def one_chunk(args):
        j, qj = args
        kj = lax.dynamic_slice_in_dim(kp, j * CHUNK, band, axis=1)
        vj = lax.dynamic_slice_in_dim(vp, j * CHUNK, band, axis=1)
        s = jnp.einsum("blhd,bkhd->bhlk", qj, kj).astype(jnp.float32) * scale + bias
        valid = (j * CHUNK + k_idx - pad) >= 0
        s = jnp.where(valid[None, None, None, :], s, -jnp.inf)
        p = jax.nn.softmax(s, axis=-1).astype(vj.dtype)
        return jnp.einsum("bhlk,bkhd->blhd", p, vj)

    out = lax.map(one_chunk, (jnp.arange(nc), qc))
    return out.transpose(1, 0, 2, 3, 4).reshape(B, S, H * dh)


def mlstm_chunkwise(q, k, v, i_pre, f_pre):
    B, S, H, d = q.shape
    nc = S // CHUNK
    k = k * (d ** -0.5)
    to_c = lambda t: t.reshape(B, nc, CHUNK, H, d).transpose(1, 0, 3, 2, 4)
    qc, kc, vc = to_c(q), to_c(k), to_c(v)
    li = i_pre.astype(jnp.float32).reshape(B, nc, CHUNK, H).transpose(1, 0, 3, 2)
    lf = jax.nn.log_sigmoid(f_pre.astype(jnp.float32)).reshape(B, nc, CHUNK, H).transpose(1, 0, 3, 2)
    a = jnp.cumsum(lf, axis=-1)
    A = a[..., -1]
    causal = jnp.tril(jnp.ones((CHUNK, CHUNK), dtype=bool))

    def step(carry, inp):
        C, n, m = carry
        qj, kj, vj, aj, Aj, lij = inp
        D = aj[..., :, None] - aj[..., None, :] + lij[..., None, :]
        D = jnp.where(causal, D, -jnp.inf)
        inter = aj + m[..., None]
        m_row = jnp.maximum(inter, jnp.max(D, axis=-1))
        w_inter = jnp.exp(inter - m_row)
        Wd = jnp.exp(D - m_row[..., None])
        qk = jnp.einsum("bhld,bhsd->bhls", qj, kj).astype(jnp.float32) * Wd
        num = w_inter[..., None] * jnp.einsum("bhld,bhde->bhle", qj, C) + jnp.einsum("bhls,bhse->bhle", qk, vj)
        den = w_inter * jnp.einsum("bhld,bhd->bhl", qj, n) + jnp.sum(qk, axis=-1)
        h = num / jnp.maximum(jnp.abs(den), jnp.exp(-m_row))[..., None]
        g = Aj[..., None] - aj + lij
        m_new = jnp.maximum(Aj + m, jnp.max(g, axis=-1))
        decay = jnp.exp(Aj + m - m_new)
        wk = jnp.exp(g - m_new[..., None])
        C_new = decay[..., None, None] * C + jnp.einsum("bhl,bhld,bhle->bhde", wk, kj, vj)
        n_new = decay[..., None] * n + jnp.einsum("bhl,bhld->bhd", wk, kj)
        return (C_new, n_new, m_new), h

    init = (jnp.zeros((B, H, d, d), jnp.float32), jnp.zeros((B, H, d), jnp.float32), jnp.zeros((B, H), jnp.float32))
    _, hs = lax.scan(step, init, (qc, kc, vc, a, A, li))
    return hs.transpose(1, 0, 3, 2, 4).reshape(B, S, H, d).astype(v.dtype)


def peer_ffn(h, w_pq, sub_keys, u_exp, v_exp):
    B, S, D = h.shape
    T = B * S
    hb = h.reshape(T // PEER_BLOCK, PEER_BLOCK, D)

    def block(xb):
        q = (xb @ w_pq).reshape(PEER_BLOCK, PEER_HEADS, 2, D_KEY // 2)
        s = jnp.einsum("thpd,hpnd->thpn", q, sub_keys).astype(jnp.float32)
        s_top, i_top = lax.top_k(s, PEER_TOPK)
        cand = (s_top[:, :, 0, :, None] + s_top[:, :, 1, None, :]).reshape(PEER_BLOCK, PEER_HEADS, PEER_TOPK * PEER_TOPK)
        best, ci = lax.top_k(cand, PEER_TOPK)
        i1 = jnp.take_along_axis(i_top[:, :, 0, :], ci // PEER_TOPK, axis=-1)
        i2 = jnp.take_along_axis(i_top[:, :, 1, :], ci % PEER_TOPK, axis=-1)
        e = i1 * N_KEYS + i2
        gate = jax.nn.softmax(best, axis=-1)
        u = u_exp[e]
        v = v_exp[e]
        act = jax.nn.gelu(jnp.einsum("thkd,td->thk", u, xb))
        return jnp.einsum("thk,thkd->td", (gate * act).astype(xb.dtype), v)

    return lax.map(block, hb).reshape(B, S, D)


def setup_inputs(seed: int = 0) -> dict:
    key = jax.random.key(seed)
    ks = jax.random.split(key, 24)
    nrm = lambda k, shape, s: jax.random.normal(k, shape, jnp.float32) * s
    L = DEPTH
    return {
        "x": nrm(ks[0], (BATCH, SEQ, D_MODEL), 1.0),
        "c": nrm(ks[1], (BATCH, D_MODEL), 1.0),
        "w_ada": nrm(ks[2], (L, D_MODEL, 6 * D_MODEL), 0.5 * D_MODEL ** -0.5),
        "b_ada": nrm(ks[3], (L, 6 * D_MODEL), 0.02),
        "norm1_g": 1.0 + nrm(ks[4], (L, D_MODEL), 0.02),
        "w_in": nrm(ks[5], (L, D_MODEL, IN_COLS), D_MODEL ** -0.5),
        "conv_w": nrm(ks[6], (L, CONV_K, 2 * B_WIDTH), CONV_K ** -0.5),
        "conv_b": nrm(ks[7], (L, 2 * B_WIDTH), 0.02),
        "b_igate": nrm(ks[8], (L, B_HEADS), 0.1),
        "b_fgate": jnp.linspace(3.0, 6.0, B_HEADS)[None, :] + nrm(ks[9], (L, B_HEADS), 0.1),
        "rel_bias": nrm(ks[10], (L, A_HEADS, 2 * REL_CLIP + 1), 0.1),
        "mlstm_norm_g": 1.0 + nrm(ks[11], (L, B_WIDTH), 0.02),
        "w_branch_a": nrm(ks[12], (L, A_WIDTH, D_MODEL), A_WIDTH ** -0.5),
        "w_branch_b": nrm(ks[13], (L, B_WIDTH, D_MODEL), B_WIDTH ** -0.5),
        "w_out": nrm(ks[14], (L, D_MODEL, D_MODEL), D_MODEL ** -0.5),
        "norm2_g": 1.0 + nrm(ks[15], (L, D_MODEL), 0.02),
        "w_peer_q": nrm(ks[16], (L, D_MODEL, PEER_HEADS * D_KEY), D_MODEL ** -0.5),
        "peer_sub_keys": nrm(ks[17], (L, PEER_HEADS, 2, N_KEYS, D_KEY // 2), (D_KEY // 2) ** -0.5),
        "peer_u": nrm(ks[18], (L, N_EXPERTS, D_MODEL), D_MODEL ** -0.5),
        "peer_v": nrm(ks[19], (L, N_EXPERTS, D_MODEL), 0.5),
        "final_g": 1.0 + nrm(ks[20], (D_MODEL,), 0.02),
    }


def reference(x, c, w_ada, b_ada, norm1_g, w_in, conv_w, conv_b, b_igate, b_fgate, rel_bias, mlstm_norm_g,
              w_branch_a, w_branch_b, w_out, norm2_g, w_peer_q, peer_sub_keys, peer_u, peer_v, final_g):
    B, S, _ = x.shape
    split_at = [int(s) for s in np.cumsum(IN_SIZES)[:-1]]
    for l in range(DEPTH):
        mod = jax.nn.silu(c) @ w_ada[l] + b_ada[l]
        sh1, sc1, gt1, sh2, sc2, gt2 = jnp.split(mod, 6, axis=-1)

        h = modulate(rmsnorm(x, norm1_g[l]), sh1, sc1)
        proj = h @ w_in[l]
        qa, ka, va, qkb, vb, ob, ib, fb, ga, gb = jnp.split(proj, split_at, axis=-1)
        qkb = jax.nn.silu(causal_depthwise_conv(qkb, conv_w[l], conv_b[l]))
        qb, kb = jnp.split(qkb, 2, axis=-1)
        heads_a = lambda t: t.reshape(B, S, A_HEADS, A_HEAD_DIM)
        heads_b = lambda t: t.reshape(B, S, B_HEADS, B_HEAD_DIM)
        y_a = chunk_band_attention(heads_a(qa), heads_a(ka), heads_a(va), rel_bias[l])
        hb = mlstm_chunkwise(heads_b(qb), heads_b(kb), heads_b(vb), ib + b_igate[l], fb + b_fgate[l])
        hb = rmsnorm(hb, mlstm_norm_g[l].reshape(B_HEADS, B_HEAD_DIM)).reshape(B, S, B_WIDTH)
        y_b = jax.nn.sigmoid(ob) * hb
        y = jax.nn.sigmoid(ga) * (y_a @ w_branch_a[l]) + jax.nn.sigmoid(gb) * (y_b @ w_branch_b[l])
        x = x + gt1[:, None, :] * (y @ w_out[l])

        h2 = modulate(rmsnorm(x, norm2_g[l]), sh2, sc2)
        x = x + gt2[:, None, :] * peer_ffn(h2, w_peer_q[l], peer_sub_keys[l], peer_u[l], peer_v[l])
    return rmsnorm(x, final_g)
```

```python
import functools

import jax
import jax.numpy as jnp
from jax import lax
from jax.experimental import pallas as pl
from jax.experimental.pallas import tpu as pltpu

F32 = jnp.float32
BF16 = jnp.bfloat16

CHUNK = 64
LEFT_CHUNKS = 8
A_HEADS = 8
A_HEAD_DIM = 64
A_WIDTH = A_HEADS * A_HEAD_DIM
REL_CLIP = 128
B_HEADS = 4
B_HEAD_DIM = 128
B_WIDTH = B_HEADS * B_HEAD_DIM
CONV_K = 4
PEER_HEADS = 8
N_KEYS = 128
PEER_TOPK = 16
D_KEY = 256
EPS = 1e-6

LANES = 128
SUBLANES = 8
NEG = -1e30
VMEM_LIMIT = 56 * 1024 * 1024

PB_GA, PB_GB, PB_QA, PB_KA, PB_VA, PB_QP, PB_KP, PB_VB, PB_OB = 0, 2, 4, 5, 6, 7, 8, 9, 10
P_COLS = 11 * 512

NT_DIMS = (((1,), (1,)), ((), ()))
TN_DIMS = (((0,), (0,)), ((), ()))


def _cparams(n_axes):
    return pltpu.CompilerParams(dimension_semantics=("arbitrary",) * n_axes, vmem_limit_bytes=VMEM_LIMIT)


def _sigmoid(x):
    return 1.0 / (1.0 + jnp.exp(-x))


def _log_sigmoid(x):
    return jnp.minimum(x, 0.0) - jnp.log(1.0 + jnp.exp(-jnp.abs(x)))


def _gelu_tanh(x):
    return 0.5 * x * (1.0 + jnp.tanh(0.7978845608028654 * (x + 0.044715 * (x * x * x))))


def _mod_kernel(c_ref, w_ref, b_ref, o_ref):
    c = c_ref[...]
    o_ref[...] = jnp.dot(c * _sigmoid(c), w_ref[...], preferred_element_type=F32,
                         precision=lax.Precision.HIGHEST) + b_ref[...]


def _mod(c, w, b):
    B, D = c.shape
    N = w.shape[1]
    tn = 512
    return pl.pallas_call(
        _mod_kernel, name="mod",
        grid=(N // tn,),
        in_specs=[pl.BlockSpec((B, D), lambda j: (0, 0)),
                  pl.BlockSpec((D, tn), lambda j: (0, j)),
                  pl.BlockSpec((1, tn), lambda j: (0, j))],
        out_specs=pl.BlockSpec((B, tn), lambda j: (0, j)),
        out_shape=jax.ShapeDtypeStruct((B, N), F32),
        compiler_params=_cparams(1),
    )(c, w, b.reshape(1, N))


def _inproj_kernel(x_ref, sh_ref, sc_ref, g_ref, w_ref, wg_ref, wgT_ref, gb_ref, gbT_ref,
                   o_ref, og_ref, ogT_ref, h_scr):
    @pl.when(pl.program_id(2) == 0)
    def _():
        x = x_ref[...]
        y = x * lax.rsqrt(jnp.mean(x * x, axis=-1, keepdims=True) + EPS) * g_ref[...]
        hb = (y * (1.0 + sc_ref[...]) + sh_ref[...]).astype(BF16)
        h_scr[...] = hb
        og_ref[...] = jnp.dot(hb, wg_ref[...], preferred_element_type=F32) + gb_ref[...]
        ogT_ref[...] = lax.dot_general(wgT_ref[...], hb, NT_DIMS, preferred_element_type=F32) + gbT_ref[...]

    o_ref[...] = jnp.dot(h_scr[...], w_ref[...], preferred_element_type=F32).astype(o_ref.dtype)


def _inproj(x, sh, sc, g, w_main, w_gate, w_gateT, gbias, gbiasT, tm, tn):
    B, S, D = x.shape
    N = w_main.shape[1]
    row = lambda b, i, j: (b, 0, 0)
    return pl.pallas_call(
        _inproj_kernel, name="inproj",
        grid=(B, S // tm, N // tn),
        in_specs=[pl.BlockSpec((None, tm, D), lambda b, i, j: (b, i, 0)),
                  pl.BlockSpec((None, 1, D), row),
                  pl.BlockSpec((None, 1, D), row),
                  pl.BlockSpec((1, D), lambda b, i, j: (0, 0)),
                  pl.BlockSpec((D, tn), lambda b, i, j: (0, j)),
                  pl.BlockSpec((D, LANES), lambda b, i, j: (0, 0)),
                  pl.BlockSpec((SUBLANES, D), lambda b, i, j: (0, 0)),
                  pl.BlockSpec((1, LANES), lambda b, i, j: (0, 0)),
                  pl.BlockSpec((SUBLANES, tm), lambda b, i, j: (0, 0))],
        out_specs=[pl.BlockSpec((None, tm, tn), lambda b, i, j: (b, i, j)),
                   pl.BlockSpec((None, tm, LANES), lambda b, i, j: (b, i, 0)),
                   pl.BlockSpec((None, SUBLANES, tm), lambda b, i, j: (b, 0, i))],
        out_shape=[jax.ShapeDtypeStruct((B, S, N), BF16),
                   jax.ShapeDtypeStruct((B, S, LANES), F32),
                   jax.ShapeDtypeStruct((B, SUBLANES, S), F32)],
        scratch_shapes=[pltpu.VMEM((tm, D), BF16)],
        compiler_params=_cparams(3),
    )(x, sh, sc, g, w_main, w_gate, w_gateT, gbias, gbiasT)


def _attn_kernel(nprev, q_ref, *refs):
    k_refs = refs[:nprev + 1]
    v_refs = refs[nprev + 1:2 * nprev + 2]
    bias_ref, o_ref = refs[2 * nprev + 2:]
    i = pl.program_id(2)
    q = q_ref[...]
    lo = lax.broadcasted_iota(jnp.int32, (1, LANES), 1) < A_HEAD_DIM
    scale = A_HEAD_DIM ** -0.5
    outs = []
    for hh in range(2):
        msk = lo if hh == 0 else jnp.logical_not(lo)
        qm = jnp.where(msk, q, jnp.zeros_like(q))
        ss = []
        for c in range(nprev + 1):
            s = lax.dot_general(qm, k_refs[c][...], NT_DIMS, preferred_element_type=F32) * scale + bias_ref[hh, c]
            if c < nprev:
                s = jnp.where(i >= nprev - c, s, NEG)
            ss.append(s)
        m = functools.reduce(jnp.maximum, [jnp.max(s, axis=-1, keepdims=True) for s in ss])
        l = None
        acc = None
        for c in range(nprev + 1):
            p = jnp.exp(ss[c] - m)
            pl_ = jnp.sum(p, axis=-1, keepdims=True)
            pv = jnp.dot(p.astype(BF16), v_refs[c][...], preferred_element_type=F32)
            l = pl_ if l is None else l + pl_
            acc = pv if acc is None else acc + pv
        outs.append(acc / l)
    o_ref[...] = jnp.where(lo, outs[0], outs[1]).astype(o_ref.dtype)


def _attn_bias(rel_bias, tq):
    nprev = LEFT_CHUNKS * CHUNK // tq
    ql = jnp.arange(tq)[:, None]
    kl = jnp.arange((nprev + 1) * tq)[None, :] - nprev * tq
    rel = jnp.clip(ql - kl, -REL_CLIP, REL_CLIP) + REL_CLIP
    qc = ql // CHUNK
    kc = jnp.floor_divide(kl, CHUNK)
    vis = (kc <= qc) & (kc >= qc - LEFT_CHUNKS)
    bias = jnp.where(vis[None], rel_bias[:, rel].astype(F32), NEG)
    return bias.reshape(A_HEADS, tq, nprev + 1, tq).transpose(0, 2, 1, 3)


def _attention(proj, rel_bias, tq):
    B, S, _ = proj.shape
    nprev = LEFT_CHUNKS * CHUNK // tq
    bias = _attn_bias(rel_bias, tq)
    npair = A_HEADS // 2
    qb, kb, vb = PB_QA * 4, PB_KA * 4, PB_VA * 4

    def kv_spec(base, c):
        back = nprev - c
        return pl.BlockSpec((None, tq, LANES), lambda p, b, i: (b, jnp.maximum(i - back, 0), base + p))

    in_specs = [pl.BlockSpec((None, tq, LANES), lambda p, b, i: (b, i, qb + p))]
    in_specs += [kv_spec(kb, c) for c in range(nprev + 1)]
    in_specs += [kv_spec(vb, c) for c in range(nprev + 1)]
    in_specs += [pl.BlockSpec((2, nprev + 1, tq, tq), lambda p, b, i: (p, 0, 0, 0))]
    return pl.pallas_call(
        functools.partial(_attn_kernel, nprev), name="attn",
        grid=(npair, B, S // tq),
        in_specs=in_specs,
        out_specs=pl.BlockSpec((None, tq, LANES), lambda p, b, i: (b, i, p)),
        out_shape=jax.ShapeDtypeStruct((B, S, A_WIDTH), BF16),
        compiler_params=_cparams(3),
    )(*([proj] * (2 * nprev + 3)), bias)


def _mlstm_kernel(rows, qp_ref, kp_ref, v_ref, ob_ref, g_ref, gT_ref, cw_ref, cb_ref, ng_ref, tril_ref, triu_ref,
                  o_ref, C_scr, n_scr, m_scr, ext_scr):
    @pl.when(pl.program_id(1) == 0)
    def _():
        C_scr[...] = jnp.zeros_like(C_scr)
        n_scr[...] = jnp.zeros_like(n_scr)
        m_scr[...] = jnp.zeros_like(m_scr)
        ext_scr[:, 0:SUBLANES, :] = jnp.zeros((2, SUBLANES, B_WIDTH), F32)

    qk = []
    for t, ref in enumerate((qp_ref, kp_ref)):
        ext_scr[t, SUBLANES:SUBLANES + rows, :] = ref[...].astype(F32)
        acc = cb_ref[:, t * B_WIDTH:(t + 1) * B_WIDTH]
        for kk in range(CONV_K):
            start = SUBLANES - (CONV_K - 1) + kk
            acc = acc + cw_ref[kk:kk + 1, t * B_WIDTH:(t + 1) * B_WIDTH] * ext_scr[t, start:start + rows, :]
        ext_scr[t, 0:SUBLANES, :] = ext_scr[t, rows:rows + SUBLANES, :]
        qk.append(acc * _sigmoid(acc))
    q_all = qk[0]
    k_all = qk[1] * (B_HEAD_DIM ** -0.5)

    g_all = g_ref[...]
    gT_all = gT_ref[...]
    li_idx = lax.broadcasted_iota(jnp.int32, (CHUNK, CHUNK), 0)
    si_idx = lax.broadcasted_iota(jnp.int32, (CHUNK, CHUNK), 1)
    causal = li_idx >= si_idx
    hi = lax.Precision.HIGHEST

    for c in range(rows // CHUNK):
        r0 = c * CHUNK
        for h in range(B_HEADS):
            l0 = h * B_HEAD_DIM
            qh = q_all[r0:r0 + CHUNK, l0:l0 + B_HEAD_DIM]
            kh = k_all[r0:r0 + CHUNK, l0:l0 + B_HEAD_DIM]
            vh = v_ref[r0:r0 + CHUNK, l0:l0 + B_HEAD_DIM]
            qh_b = qh.astype(BF16)
            li_b = jnp.broadcast_to(g_all[r0:r0 + CHUNK, h:h + 1], (CHUNK, LANES))
            lf_b = _log_sigmoid(jnp.broadcast_to(g_all[r0:r0 + CHUNK, B_HEADS + h:B_HEADS + h + 1], (CHUNK, LANES)))
            li_r = jnp.broadcast_to(gT_all[h:h + 1, r0:r0 + CHUNK], (CHUNK, CHUNK))
            lf_r = _log_sigmoid(jnp.broadcast_to(gT_all[B_HEADS + h:B_HEADS + h + 1, r0:r0 + CHUNK], (CHUNK, CHUNK)))
            a_b = jnp.dot(tril_ref[...], lf_b, preferred_element_type=F32, precision=hi)
            a_r = jnp.dot(lf_r, triu_ref[...], preferred_element_type=F32, precision=hi)
            D = jnp.where(causal, a_b[:, :CHUNK] - a_r + li_r, NEG)

            m_prev = m_scr[h, 0:1, :]
            inter = a_b + m_prev
            m_row = jnp.maximum(inter, jnp.max(D, axis=-1, keepdims=True))
            w_inter = jnp.exp(inter - m_row)
            Wd = jnp.exp(D - m_row[:, :CHUNK])
            qkw = lax.dot_general(qh_b, kh.astype(BF16), NT_DIMS, preferred_element_type=F32) * Wd
            num = (w_inter * jnp.dot(qh_b, C_scr[h].astype(BF16), preferred_element_type=F32)
                   + jnp.dot(qkw.astype(BF16), vh, preferred_element_type=F32))
            qn = jnp.sum(qh * n_scr[h, 0:1, :], axis=-1, keepdims=True)
            den = w_inter[:, 0:1] * qn + jnp.sum(qkw, axis=-1, keepdims=True)
            hh = num / jnp.maximum(jnp.abs(den), jnp.exp(-m_row[:, 0:1]))

            A_row = a_b[CHUNK - 1:CHUNK, :]
            g_b = A_row - a_b + li_b
            m_new = jnp.maximum(A_row + m_prev, jnp.max(g_b, axis=0, keepdims=True))
            decay = jnp.exp(A_row + m_prev - m_new)
            kw = jnp.exp(g_b - m_new) * kh
            C_scr[h] = decay * C_scr[h] + lax.dot_general(kw.astype(BF16), vh, TN_DIMS, preferred_element_type=F32)
            n_scr[h] = jnp.broadcast_to(decay * n_scr[h, 0:1, :] + jnp.sum(kw, axis=0, keepdims=True),
                                        (SUBLANES, LANES))
            m_scr[h] = jnp.broadcast_to(m_new, (SUBLANES, LANES))

            hn = hh * lax.rsqrt(jnp.mean(hh * hh, axis=-1, keepdims=True) + EPS) * ng_ref[:, l0:l0 + B_HEAD_DIM]
            ob = ob_ref[r0:r0 + CHUNK, l0:l0 + B_HEAD_DIM].astype(F32)
            o_ref[r0:r0 + CHUNK, l0:l0 + B_HEAD_DIM] = (_sigmoid(ob) * hn).astype(o_ref.dtype)


def _mlstm(proj, gates, gatesT, conv_w, conv_b, norm_g, rows):
    B, S, _ = proj.shape
    tril = jnp.tril(jnp.ones((CHUNK, CHUNK), F32))
    blk = lambda cb: pl.BlockSpec((None, rows, B_WIDTH), lambda b, j: (b, j, cb))
    full = lambda shape: pl.BlockSpec(shape, lambda b, j: (0,) * len(shape))
    return pl.pallas_call(
        functools.partial(_mlstm_kernel, rows), name="mlstm",
        grid=(B, S // rows),
        in_specs=[blk(PB_QP), blk(PB_KP), blk(PB_VB), blk(PB_OB),
                  pl.BlockSpec((None, rows, LANES), lambda b, j: (b, j, 0)),
                  pl.BlockSpec((None, SUBLANES, rows), lambda b, j: (b, 0, j)),
                  full((CONV_K, 2 * B_WIDTH)), full((1, 2 * B_WIDTH)), full((1, B_WIDTH)),
                  full((CHUNK, CHUNK)), full((CHUNK, CHUNK))],
        out_specs=pl.BlockSpec((None, rows, B_WIDTH), lambda b, j: (b, j, 0)),
        out_shape=jax.ShapeDtypeStruct((B, S, B_WIDTH), BF16),
        scratch_shapes=[pltpu.VMEM((B_HEADS, B_HEAD_DIM, B_HEAD_DIM), F32),
                        pltpu.VMEM((B_HEADS, SUBLANES, LANES), F32),
                        pltpu.VMEM((B_HEADS, SUBLANES, LANES), F32),
                        pltpu.VMEM((2, rows + SUBLANES, B_WIDTH), F32)],
        compiler_params=_cparams(2),
    )(proj, proj, proj, proj, gates, gatesT, conv_w, conv_b.reshape(1, -1), norm_g.reshape(1, -1), tril, tril.T)


def _merge_kernel(ya_ref, yb_ref, ga_ref, gb_ref, x_ref, gt_ref, wa_ref, wb_ref, wo_ref, n2_ref, sh_ref, sc_ref,
                  x1_ref, h2_ref):
    ya = jnp.dot(ya_ref[...], wa_ref[...], preferred_element_type=F32)
    yb = jnp.dot(yb_ref[...], wb_ref[...], preferred_element_type=F32)
    y = _sigmoid(ga_ref[...].astype(F32)) * ya + _sigmoid(gb_ref[...].astype(F32)) * yb
    x1 = x_ref[...] + gt_ref[...] * jnp.dot(y.astype(BF16), wo_ref[...], preferred_element_type=F32)
    x1_ref[...] = x1
    yn = x1 * lax.rsqrt(jnp.mean(x1 * x1, axis=-1, keepdims=True) + EPS) * n2_ref[...]
    h2_ref[...] = (yn * (1.0 + sc_ref[...]) + sh_ref[...]).astype(h2_ref.dtype)


def _merge(y_a, y_b, proj, x, gt1, wa, wb, wo, n2g, sh2, sc2, tm):
    B, S, D = x.shape
    row = pl.BlockSpec((None, 1, D), lambda b, i: (b, 0, 0))
    full = lambda shape: pl.BlockSpec(shape, lambda b, i: (0,) * len(shape))
    return pl.pallas_call(
        _merge_kernel, name="merge",
        grid=(B, S // tm),
        in_specs=[pl.BlockSpec((None, tm, A_WIDTH), lambda b, i: (b, i, 0)),
                  pl.BlockSpec((None, tm, B_WIDTH), lambda b, i: (b, i, 0)),
                  pl.BlockSpec((None, tm, D), lambda b, i: (b, i, PB_GA // 2)),
                  pl.BlockSpec((None, tm, D), lambda b, i: (b, i, PB_GB // 2)),
                  pl.BlockSpec((None, tm, D), lambda b, i: (b, i, 0)),
                  row, full(wa.shape), full(wb.shape), full(wo.shape), full((1, D)), row, row],
        out_specs=[pl.BlockSpec((None, tm, D), lambda b, i: (b, i, 0)),
                   pl.BlockSpec((None, tm, D), lambda b, i: (b, i, 0))],
        out_shape=[jax.ShapeDtypeStruct((B, S, D), F32), jax.ShapeDtypeStruct((B, S, D), BF16)],
        compiler_params=_cparams(2),
    )(y_a, y_b, proj, proj, x, gt1, wa, wb, wo, n2g, sh2, sc2)


def _fold_kernel(sk_ref, wq_ref, o_ref):
    o_ref[...] = lax.dot_general(sk_ref[...], wq_ref[...], NT_DIMS, preferred_element_type=F32,
                                 precision=lax.Precision.HIGHEST).astype(o_ref.dtype)


def _fold(w_pq, sub_keys):
    D = w_pq.shape[0]
    dh = D_KEY // 2
    nhp = PEER_HEADS * 2
    return pl.pallas_call(
        _fold_kernel, name="fold",
        grid=(nhp,),
        in_specs=[pl.BlockSpec((None, N_KEYS, dh), lambda j: (j, 0, 0)),
                  pl.BlockSpec((D, dh), lambda j: (0, j))],
        out_specs=pl.BlockSpec((N_KEYS, D), lambda j: (j, 0)),
        out_shape=jax.ShapeDtypeStruct((nhp * N_KEYS, D), BF16),
        compiler_params=_cparams(1),
    )(sub_keys.reshape(nhp, N_KEYS, dh), w_pq)


_CELLS = [(r1, r2) for r1 in range(PEER_TOPK) for r2 in range(PEER_TOPK) if (r1 + 1) * (r2 + 1) <= PEER_TOPK]


def _select_kernel(tt, h2_ref, ws_ref, e1_ref, n1_ref, e2_ref, r2_ref,
                   s_scr, r1_scr, e1_scr, v_scr, n_scr, z_scr):
    nlt = tt // LANES
    s = lax.dot_general(ws_ref[...], h2_ref[...], NT_DIMS, preferred_element_type=F32)
    for lt in range(nlt):
        s_scr[lt] = s[:, lt * LANES:(lt + 1) * LANES]

    def extract(it, carry):
        h = it // nlt
        lt = it % nlt
        for p in range(2):
            sv = s_scr[lt, pl.ds(pl.multiple_of((h * 2 + p) * N_KEYS, N_KEYS), N_KEYS), :]
            work = sv
            rank = jnp.full((N_KEYS, LANES), float(PEER_TOPK), F32)
            top = None
            for r in range(PEER_TOPK):
                m = jnp.max(work, axis=0, keepdims=True)
                if r == 0:
                    top = m
                eq = work == m
                rank = jnp.where(eq, float(r), rank)
                work = jnp.where(eq, NEG, work)
                v_scr[p, r, lt, pl.ds(h, 1), :] = m
            e = jnp.exp(sv - top)
            if p == 0:
                r1_scr[h, lt] = rank
                e1_scr[h, lt] = e
            else:
                r2_ref[h, lt] = rank.astype(r2_ref.dtype)
                e2_ref[h, lt] = e.astype(e2_ref.dtype)
        return carry

    lax.fori_loop(0, PEER_HEADS * nlt, extract, 0)

    def cells(lt, carry):
        v1 = [v_scr[0, r, lt] for r in range(PEER_TOPK)]
        v2 = [v_scr[1, r, lt] for r in range(PEER_TOPK)]
        vals = [v1[r1] + v2[r2] for (r1, r2) in _CELLS]
        nc = len(_CELLS)
        static = [0] * nc
        dyn = [None] * nc
        for a in range(nc):
            for b in range(a + 1, nc):
                (a1, a2), (b1, b2) = _CELLS[a], _CELLS[b]
                if a1 <= b1 and a2 <= b2:
                    static[b] += 1
                else:
                    ge = vals[a] >= vals[b]
                    inc_b = jnp.where(ge, 1.0, 0.0)
                    dyn[b] = inc_b if dyn[b] is None else dyn[b] + inc_b
                    inc_a = 1.0 - inc_b
                    dyn[a] = inc_a if dyn[a] is None else dyn[a] + inc_a
        cnt = [None] * PEER_TOPK
        z = None
        for ci, (r1, r2) in enumerate(_CELLS):
            rk = static[ci] + (dyn[ci] if dyn[ci] is not None else 0.0)
            sel = jnp.where(rk < float(PEER_TOPK), 1.0, 0.0)
            cnt[r1] = sel if cnt[r1] is None else cnt[r1] + sel
            zt = sel * jnp.exp(vals[ci] - vals[0])
            z = zt if z is None else z + zt
        for r in range(PEER_TOPK):
            n_scr[r, lt] = cnt[r]
        z_scr[lt] = 1.0 / z
        return carry

    lax.fori_loop(0, nlt, cells, 0)

    for h in range(PEER_HEADS):
        for lt in range(nlt):
            r1 = r1_scr[h, lt]
            n1 = jnp.zeros((N_KEYS, LANES), F32)
            for r in range(PEER_TOPK):
                n1 = jnp.where(r1 == float(r), n_scr[r, lt, h:h + 1, :], n1)
            e1 = e1_scr[h, lt] * z_scr[lt, h:h + 1, :]
            n1_ref[lt, pl.ds(h, N_KEYS, stride=PEER_HEADS), :] = n1
            e1_ref[lt, pl.ds(h, N_KEYS, stride=PEER_HEADS), :] = e1


def _select(h2, ws, tt):
    T, D = h2.shape
    nlt = tt // LANES
    rows = N_KEYS * PEER_HEADS
    return pl.pallas_call(
        functools.partial(_select_kernel, tt), name="select",
        grid=(T // tt,),
        in_specs=[pl.BlockSpec((tt, D), lambda i: (i, 0)),
                  pl.BlockSpec(ws.shape, lambda i: (0, 0))],
        out_specs=[pl.BlockSpec((nlt, rows, LANES), lambda i: (i, 0, 0)),
                   pl.BlockSpec((nlt, rows, LANES), lambda i: (i, 0, 0)),
                   pl.BlockSpec((PEER_HEADS, nlt, N_KEYS, LANES), lambda i: (0, i, 0, 0)),
                   pl.BlockSpec((PEER_HEADS, nlt, N_KEYS, LANES), lambda i: (0, i, 0, 0))],
        out_shape=[jax.ShapeDtypeStruct((T // LANES, rows, LANES), F32),
                   jax.ShapeDtypeStruct((T // LANES, rows, LANES), F32),
                   jax.ShapeDtypeStruct((PEER_HEADS, T // LANES, N_KEYS, LANES), BF16),
                   jax.ShapeDtypeStruct((PEER_HEADS, T // LANES, N_KEYS, LANES), BF16)],
        scratch_shapes=[pltpu.VMEM((nlt, 2 * PEER_HEADS * N_KEYS, LANES), F32),
                        pltpu.VMEM((PEER_HEADS, nlt, N_KEYS, LANES), F32),
                        pltpu.VMEM((PEER_HEADS, nlt, N_KEYS, LANES), F32),
                        pltpu.VMEM((2, PEER_TOPK, nlt, PEER_HEADS, LANES), F32),
                        pltpu.VMEM((PEER_TOPK, nlt, PEER_HEADS, LANES), F32),
                        pltpu.VMEM((nlt, PEER_HEADS, LANES), F32)],
        compiler_params=_cparams(1),
    )(h2, ws)


def _dense_kernel(eb, final_norm, h2_ref, u_ref, vT_ref, e1_ref, n1_ref, e2_ref, r2_ref, x1_ref, gt_ref, fg_ref,
                  o_ref, a_scr, w_scr, acc_scr):
    j = pl.program_id(2)

    @pl.when(j == 0)
    def _():
        acc_scr[...] = jnp.zeros_like(acc_scr)

    tt = h2_ref.shape[0]
    a_scr[...] = lax.dot_general(u_ref[...], h2_ref[...], NT_DIMS, preferred_element_type=F32)

    def key_block(k, carry):
        rows = pl.ds(pl.multiple_of(k * N_KEYS, N_KEYS), N_KEYS)
        heads = pl.ds(pl.multiple_of(k * PEER_HEADS, PEER_HEADS), PEER_HEADS)
        for lt in range(tt // LANES):
            lanes = slice(lt * LANES, (lt + 1) * LANES)
            e1_all = e1_ref[lt, heads, :]
            n1_all = n1_ref[lt, heads, :]
            gate = None
            for h in range(PEER_HEADS):
                e1 = jnp.broadcast_to(e1_all[h:h + 1, :], (N_KEYS, LANES)).astype(BF16)
                n1 = jnp.broadcast_to(n1_all[h:h + 1, :], (N_KEYS, LANES)).astype(BF16)
                term = jnp.where(r2_ref[h, lt] < n1, e1 * e2_ref[h, lt], jnp.zeros((), BF16))
                gate = term if gate is None else gate + term
            act = _gelu_tanh(a_scr[rows, lanes])
            w_scr[rows, lanes] = (gate.astype(F32) * act).astype(BF16)
        return carry

    lax.fori_loop(0, eb // N_KEYS, key_block, 0)
    acc_scr[...] += jnp.dot(vT_ref[...], w_scr[...], preferred_element_type=F32)

    @pl.when(j == pl.num_programs(2) - 1)
    def _():
        x2 = x1_ref[...] + gt_ref[...] * acc_scr[...].T
        if final_norm:
            x2 = x2 * lax.rsqrt(jnp.mean(x2 * x2, axis=-1, keepdims=True) + EPS) * fg_ref[...]
        o_ref[...] = x2


def _dense(h2, u_bf, vT_bf, e1n, n1, e2, r2, x1, gt2, fg, final_norm, tt, eb):
    B, S, D = x1.shape
    E = u_bf.shape[0]
    nt = S // tt
    rb = eb // N_KEYS * PEER_HEADS
    return pl.pallas_call(
        functools.partial(_dense_kernel, eb, final_norm), name="dense",
        grid=(B, nt, E // eb),
        in_specs=[pl.BlockSpec((None, tt, D), lambda b, i, j: (b, i, 0)),
                  pl.BlockSpec((eb, D), lambda b, i, j: (j, 0)),
                  pl.BlockSpec((D, eb), lambda b, i, j: (0, j)),
                  pl.BlockSpec((tt // LANES, rb, LANES), lambda b, i, j: (b * nt + i, j, 0)),
                  pl.BlockSpec((tt // LANES, rb, LANES), lambda b, i, j: (b * nt + i, j, 0)),
                  pl.BlockSpec((PEER_HEADS, tt // LANES, N_KEYS, LANES), lambda b, i, j: (0, b * nt + i, 0, 0)),
                  pl.BlockSpec((PEER_HEADS, tt // LANES, N_KEYS, LANES), lambda b, i, j: (0, b * nt + i, 0, 0)),
                  pl.BlockSpec((None, tt, D), lambda b, i, j: (b, i, 0)),
                  pl.BlockSpec((None, 1, D), lambda b, i, j: (b, 0, 0)),
                  pl.BlockSpec((1, D), lambda b, i, j: (0, 0))],
        out_specs=pl.BlockSpec((None, tt, D), lambda b, i, j: (b, i, 0)),
        out_shape=jax.ShapeDtypeStruct((B, S, D), F32),
        scratch_shapes=[pltpu.VMEM((eb, tt), F32), pltpu.VMEM((eb, tt), BF16), pltpu.VMEM((D, tt), F32)],
        compiler_params=_cparams(3),
    )(h2, u_bf, vT_bf, e1n, n1, e2, r2, x1, gt2, fg)


def _tile(n, pref):
    return pref if n % pref == 0 else n


def kernel(x, c, w_ada, b_ada, norm1_g, w_in, conv_w, conv_b, b_igate, b_fgate, rel_bias, mlstm_norm_g,
           w_branch_a, w_branch_b, w_out, norm2_g, w_peer_q, peer_sub_keys, peer_u, peer_v, final_g):
    B, S, D = x.shape
    depth = w_ada.shape[0]
    for l in range(depth):
        mod = _mod(c, w_ada[l], b_ada[l])
        sh1, sc1, gt1, sh2, sc2, gt2 = [m.reshape(B, 1, D) for m in jnp.split(mod, 6, axis=-1)]

        o = [0]
        for sz in (A_WIDTH, A_WIDTH, A_WIDTH, 2 * B_WIDTH, B_WIDTH, B_WIDTH, B_HEADS, B_HEADS, D, D):
            o.append(o[-1] + sz)
        wl = w_in[l]
        seg = lambda i: wl[:, o[i]:o[i + 1]]
        w_main = jnp.concatenate([seg(8), seg(9), seg(0), seg(1), seg(2), seg(3), seg(4), seg(5)], axis=1).astype(BF16)
        w_gate = jnp.concatenate([seg(6), seg(7), jnp.zeros((D, LANES - 2 * B_HEADS), F32)], axis=1).astype(BF16)
        w_gateT = w_gate[:, :SUBLANES].T
        gb = jnp.concatenate([b_igate[l], b_fgate[l]]).astype(F32)
        tm = _tile(S, 1024)
        gbias = jnp.concatenate([gb, jnp.zeros((LANES - 2 * B_HEADS,), F32)]).reshape(1, LANES)
        gbiasT = jnp.broadcast_to(gb[:, None], (SUBLANES, tm))
        proj, gates, gatesT = _inproj(x, sh1, sc1, norm1_g[l].reshape(1, D), w_main, w_gate, w_gateT,
                                      gbias, gbiasT, tm, 512)

        y_a = _attention(proj, rel_bias[l], _tile(S, 256))
        y_b = _mlstm(proj, gates, gatesT, conv_w[l], conv_b[l], mlstm_norm_g[l], _tile(S, 128))
        x1, h2 = _merge(y_a, y_b, proj, x, gt1, w_branch_a[l].astype(BF16), w_branch_b[l].astype(BF16),
                        w_out[l].astype(BF16), norm2_g[l].reshape(1, D), sh2, sc2, _tile(S, 512))

        ws = _fold(w_peer_q[l], peer_sub_keys[l])
        e1n, n1, e2, r2 = _select(h2.reshape(B * S, D), ws, _tile(B * S, 512))
        x = _dense(h2, peer_u[l].astype(BF16), peer_v[l].T.astype(BF16), e1n, n1, e2, r2, x1, gt2,
                   final_g.reshape(1, D), l == depth - 1, _tile(S, 1024), 1024)
    return x
```

```python
import functools

import jax
import jax.numpy as jnp
from jax import lax
from jax.experimental import pallas as pl
from jax.experimental.pallas import tpu as pltpu

F32 = jnp.float32
BF16 = jnp.bfloat16

CHUNK = 64
LEFT_CHUNKS = 8
A_HEADS = 8
A_HEAD_DIM = 64
A_WIDTH = A_HEADS * A_HEAD_DIM
REL_CLIP = 128
B_HEADS = 4
B_HEAD_DIM = 128
B_WIDTH = B_HEADS * B_HEAD_DIM
CONV_K = 4
PEER_HEADS = 8
N_KEYS = 128
PEER_TOPK = 16
D_KEY = 256
EPS = 1e-6

LANES = 128
SUBLANES = 8
NEG = -1e30
VMEM_LIMIT = 56 * 1024 * 1024

PB_GA, PB_GB, PB_QA, PB_KA, PB_VA, PB_QP, PB_KP, PB_VB, PB_OB = 0, 2, 4, 5, 6, 7, 8, 9, 10
P_COLS = 11 * 512

NT_DIMS = (((1,), (1,)), ((), ()))
TN_DIMS = (((0,), (0,)), ((), ()))


def _cparams(n_axes, flags=None):
    return pltpu.CompilerParams(dimension_semantics=("arbitrary",) * n_axes, vmem_limit_bytes=VMEM_LIMIT,
                                flags=flags)


def _sigmoid(x):
    return 1.0 / (1.0 + jnp.exp(-x))


def _log_sigmoid(x):
    return jnp.minimum(x, 0.0) - jnp.log(1.0 + jnp.exp(-jnp.abs(x)))


def _mod_kernel(c_ref, w_ref, b_ref, o_ref):
    c = c_ref[...]
    o_ref[...] = jnp.dot(c * _sigmoid(c), w_ref[...], preferred_element_type=F32,
                         precision=lax.Precision.HIGHEST) + b_ref[...]


def _mod(c, w, b):
    B, D = c.shape
    N = w.shape[1]
    tn = 512
    return pl.pallas_call(
        _mod_kernel, name="mod",
        grid=(N // tn,),
        in_specs=[pl.BlockSpec((B, D), lambda j: (0, 0)),
                  pl.BlockSpec((D, tn), lambda j: (0, j)),
                  pl.BlockSpec((1, tn), lambda j: (0, j))],
        out_specs=pl.BlockSpec((B, tn), lambda j: (0, j)),
        out_shape=jax.ShapeDtypeStruct((B, N), F32),
        compiler_params=_cparams(1),
    )(c, w, b.reshape(1, N))


def _inproj_kernel(x_ref, sh_ref, sc_ref, g_ref, w_ref, wg_ref, wgT_ref, gb_ref, gbT_ref,
                   o_ref, og_ref, ogT_ref, h_scr):
    @pl.when(pl.program_id(2) == 0)
    def _():
        x = x_ref[...]
        y = x * lax.rsqrt(jnp.mean(x * x, axis=-1, keepdims=True) + EPS) * g_ref[...]
        hb = (y * (1.0 + sc_ref[...]) + sh_ref[...]).astype(BF16)
        h_scr[...] = hb
        og_ref[...] = jnp.dot(hb, wg_ref[...], preferred_element_type=F32) + gb_ref[...]
        ogT_ref[...] = lax.dot_general(wgT_ref[...], hb, NT_DIMS, preferred_element_type=F32) + gbT_ref[...]

    o_ref[...] = jnp.dot(h_scr[...], w_ref[...], preferred_element_type=F32).astype(o_ref.dtype)


def _inproj(x, sh, sc, g, w_main, w_gate, w_gateT, gbias, gbiasT, tm, tn):
    B, S, D = x.shape
    N = w_main.shape[1]
    row = lambda b, i, j: (b, 0, 0)
    return pl.pallas_call(
        _inproj_kernel, name="inproj",
        grid=(B, S // tm, N // tn),
        in_specs=[pl.BlockSpec((None, tm, D), lambda b, i, j: (b, i, 0)),
                  pl.BlockSpec((None, 1, D), row),
                  pl.BlockSpec((None, 1, D), row),
                  pl.BlockSpec((1, D), lambda b, i, j: (0, 0)),
                  pl.BlockSpec((D, tn), lambda b, i, j: (0, j)),
                  pl.BlockSpec((D, LANES), lambda b, i, j: (0, 0)),
                  pl.BlockSpec((SUBLANES, D), lambda b, i, j: (0, 0)),
                  pl.BlockSpec((1, LANES), lambda b, i, j: (0, 0)),
                  pl.BlockSpec((SUBLANES, tm), lambda b, i, j: (0, 0))],
        out_specs=[pl.BlockSpec((None, tm, tn), lambda b, i, j: (b, i, j)),
                   pl.BlockSpec((None, tm, LANES), lambda b, i, j: (b, i, 0)),
                   pl.BlockSpec((None, SUBLANES, tm), lambda b, i, j: (b, 0, i))],
        out_shape=[jax.ShapeDtypeStruct((B, S, N), BF16),
                   jax.ShapeDtypeStruct((B, S, LANES), F32),
                   jax.ShapeDtypeStruct((B, SUBLANES, S), F32)],
        scratch_shapes=[pltpu.VMEM((tm, D), BF16)],
        compiler_params=_cparams(3),
    )(x, sh, sc, g, w_main, w_gate, w_gateT, gbias, gbiasT)


def _attn_kernel(nprev, q_ref, *refs):
    k_refs = refs[:nprev + 1]
    v_refs = refs[nprev + 1:2 * nprev + 2]
    bias_ref, o_ref = refs[2 * nprev + 2:]
    i = pl.program_id(2)
    q = q_ref[...]
    lo = lax.broadcasted_iota(jnp.int32, (1, LANES), 1) < A_HEAD_DIM
    scale = A_HEAD_DIM ** -0.5
    outs = []
    for hh in range(2):
        msk = lo if hh == 0 else jnp.logical_not(lo)
        qm = jnp.where(msk, q, jnp.zeros_like(q))
        ss = []
        for c in range(nprev + 1):
            s = lax.dot_general(qm, k_refs[c][...], NT_DIMS, preferred_element_type=F32) * scale + bias_ref[hh, c]
            if c < nprev:
                s = jnp.where(i >= nprev - c, s, NEG)
            ss.append(s)
        m = functools.reduce(jnp.maximum, [jnp.max(s, axis=-1, keepdims=True) for s in ss])
        l = None
        acc = None
        for c in range(nprev + 1):
            p = jnp.exp(ss[c] - m)
            pl_ = jnp.sum(p, axis=-1, keepdims=True)
            pv = jnp.dot(p.astype(BF16), v_refs[c][...], preferred_element_type=F32)
            l = pl_ if l is None else l + pl_
            acc = pv if acc is None else acc + pv
        outs.append(acc / l)
    o_ref[...] = jnp.where(lo, outs[0], outs[1]).astype(o_ref.dtype)


def _attn_bias(rel_bias, tq):
    nprev = LEFT_CHUNKS * CHUNK // tq
    nk = (nprev + 1) * tq
    ql = jnp.arange(tq)[:, None]
    kl = jnp.arange(nk)[None, :] - nprev * tq
    L = nk + tq
    delta = jnp.arange(L)
    delta = jnp.where(delta < nk, delta, delta - L)
    w = rel_bias[:, jnp.clip(nprev * tq - delta, -REL_CLIP, REL_CLIP) + REL_CLIP].astype(F32)
    toep = jnp.tile(w, (1, tq))[:, :tq * (L - 1)].reshape(A_HEADS, tq, L - 1)[:, :, :nk]
    qc = ql // CHUNK
    kc = jnp.floor_divide(kl, CHUNK)
    vis = (kc <= qc) & (kc >= qc - LEFT_CHUNKS)
    bias = jnp.where(vis[None], toep, NEG)
    return bias.reshape(A_HEADS, tq, nprev + 1, tq).transpose(0, 2, 1, 3)


def _attention(proj, rel_bias, tq):
    B, S, _ = proj.shape
    nprev = LEFT_CHUNKS * CHUNK // tq
    bias = _attn_bias(rel_bias, tq)
    npair = A_HEADS // 2
    qb, kb, vb = PB_QA * 4, PB_KA * 4, PB_VA * 4

    def kv_spec(base, c):
        back = nprev - c
        return pl.BlockSpec((None, tq, LANES), lambda p, b, i: (b, jnp.maximum(i - back, 0), base + p))

    in_specs = [pl.BlockSpec((None, tq, LANES), lambda p, b, i: (b, i, qb + p))]
    in_specs += [kv_spec(kb, c) for c in range(nprev + 1)]
    in_specs += [kv_spec(vb, c) for c in range(nprev + 1)]
    in_specs += [pl.BlockSpec((2, nprev + 1, tq, tq), lambda p, b, i: (p, 0, 0, 0))]
    return pl.pallas_call(
        functools.partial(_attn_kernel, nprev), name="attn",
        grid=(npair, B, S // tq),
        in_specs=in_specs,
        out_specs=pl.BlockSpec((None, tq, LANES), lambda p, b, i: (b, i, p)),
        out_shape=jax.ShapeDtypeStruct((B, S, A_WIDTH), BF16),
        compiler_params=_cparams(3),
    )(*([proj] * (2 * nprev + 3)), bias)


def _mlstm_kernel(rows, qp_ref, kp_ref, v_ref, ob_ref, g_ref, gT_ref, cw_ref, cb_ref, ng_ref, tril_ref, triu_ref,
                  o_ref, C_scr, n_scr, m_scr, ext_scr):
    @pl.when(pl.program_id(1) == 0)
    def _():
        C_scr[...] = jnp.zeros_like(C_scr)
        n_scr[...] = jnp.zeros_like(n_scr)
        m_scr[...] = jnp.zeros_like(m_scr)
        ext_scr[:, 0:SUBLANES, :] = jnp.zeros((2, SUBLANES, B_WIDTH), F32)

    qk = []
    for t, ref in enumerate((qp_ref, kp_ref)):
        ext_scr[t, SUBLANES:SUBLANES + rows, :] = ref[...].astype(F32)
        acc = cb_ref[:, t * B_WIDTH:(t + 1) * B_WIDTH]
        for kk in range(CONV_K):
            start = SUBLANES - (CONV_K - 1) + kk
            acc = acc + cw_ref[kk:kk + 1, t * B_WIDTH:(t + 1) * B_WIDTH] * ext_scr[t, start:start + rows, :]
        ext_scr[t, 0:SUBLANES, :] = ext_scr[t, rows:rows + SUBLANES, :]
        qk.append(acc * _sigmoid(acc))
    q_all = qk[0]
    k_all = qk[1] * (B_HEAD_DIM ** -0.5)

    g_all = g_ref[...]
    gT_all = gT_ref[...]
    li_idx = lax.broadcasted_iota(jnp.int32, (CHUNK, CHUNK), 0)
    si_idx = lax.broadcasted_iota(jnp.int32, (CHUNK, CHUNK), 1)
    causal = li_idx >= si_idx
    hi = lax.Precision.HIGHEST

    for c in range(rows // CHUNK):
        r0 = c * CHUNK
        for h in range(B_HEADS):
            l0 = h * B_HEAD_DIM
            qh = q_all[r0:r0 + CHUNK, l0:l0 + B_HEAD_DIM]
            kh = k_all[r0:r0 + CHUNK, l0:l0 + B_HEAD_DIM]
            vh = v_ref[r0:r0 + CHUNK, l0:l0 + B_HEAD_DIM]
            qh_b = qh.astype(BF16)
            li_b = jnp.broadcast_to(g_all[r0:r0 + CHUNK, h:h + 1], (CHUNK, LANES))
            lf_b = _log_sigmoid(jnp.broadcast_to(g_all[r0:r0 + CHUNK, B_HEADS + h:B_HEADS + h + 1], (CHUNK, LANES)))
            li_r = jnp.broadcast_to(gT_all[h:h + 1, r0:r0 + CHUNK], (CHUNK, CHUNK))
            lf_r = _log_sigmoid(jnp.broadcast_to(gT_all[B_HEADS + h:B_HEADS + h + 1, r0:r0 + CHUNK], (CHUNK, CHUNK)))
            a_b = jnp.dot(tril_ref[...], lf_b, preferred_element_type=F32, precision=hi)
            a_r = jnp.dot(lf_r, triu_ref[...], preferred_element_type=F32, precision=hi)
            D = jnp.where(causal, a_b[:, :CHUNK] - a_r + li_r, NEG)

            m_prev = m_scr[h, 0:1, :]
            inter = a_b + m_prev
            m_row = jnp.maximum(inter, jnp.max(D, axis=-1, keepdims=True))
            w_inter = jnp.exp(inter - m_row)
            Wd = jnp.exp(D - m_row[:, :CHUNK])
            qkw = lax.dot_general(qh_b, kh.astype(BF16), NT_DIMS, preferred_element_type=F32) * Wd
            num = (w_inter * jnp.dot(qh_b, C_scr[h].astype(BF16), preferred_element_type=F32)
                   + jnp.dot(qkw.astype(BF16), vh, preferred_element_type=F32))
            qn = jnp.sum(qh * n_scr[h, 0:1, :], axis=-1, keepdims=True)
            den = w_inter[:, 0:1] * qn + jnp.sum(qkw, axis=-1, keepdims=True)
            hh = num / jnp.maximum(jnp.abs(den), jnp.exp(-m_row[:, 0:1]))

            A_row = a_b[CHUNK - 1:CHUNK, :]
            g_b = A_row - a_b + li_b
            m_new = jnp.maximum(A_row + m_prev, jnp.max(g_b, axis=0, keepdims=True))
            decay = jnp.exp(A_row + m_prev - m_new)
            kw = jnp.exp(g_b - m_new) * kh
            C_scr[h] = decay * C_scr[h] + lax.dot_general(kw.astype(BF16), vh, TN_DIMS, preferred_element_type=F32)
            n_scr[h] = jnp.broadcast_to(decay * n_scr[h, 0:1, :] + jnp.sum(kw, axis=0, keepdims=True),
                                        (SUBLANES, LANES))
            m_scr[h] = jnp.broadcast_to(m_new, (SUBLANES, LANES))

            hn = hh * lax.rsqrt(jnp.mean(hh * hh, axis=-1, keepdims=True) + EPS) * ng_ref[:, l0:l0 + B_HEAD_DIM]
            ob = ob_ref[r0:r0 + CHUNK, l0:l0 + B_HEAD_DIM].astype(F32)
            o_ref[r0:r0 + CHUNK, l0:l0 + B_HEAD_DIM] = (_sigmoid(ob) * hn).astype(o_ref.dtype)


def _mlstm(proj, gates, gatesT, conv_w, conv_b, norm_g, rows):
    B, S, _ = proj.shape
    tril = jnp.tril(jnp.ones((CHUNK, CHUNK), F32))
    blk = lambda cb: pl.BlockSpec((None, rows, B_WIDTH), lambda b, j: (b, j, cb))
    full = lambda shape: pl.BlockSpec(shape, lambda b, j: (0,) * len(shape))
    return pl.pallas_call(
        functools.partial(_mlstm_kernel, rows), name="mlstm",
        grid=(B, S // rows),
        in_specs=[blk(PB_QP), blk(PB_KP), blk(PB_VB), blk(PB_OB),
                  pl.BlockSpec((None, rows, LANES), lambda b, j: (b, j, 0)),
                  pl.BlockSpec((None, SUBLANES, rows), lambda b, j: (b, 0, j)),
                  full((CONV_K, 2 * B_WIDTH)), full((1, 2 * B_WIDTH)), full((1, B_WIDTH)),
                  full((CHUNK, CHUNK)), full((CHUNK, CHUNK))],
        out_specs=pl.BlockSpec((None, rows, B_WIDTH), lambda b, j: (b, j, 0)),
        out_shape=jax.ShapeDtypeStruct((B, S, B_WIDTH), BF16),
        scratch_shapes=[pltpu.VMEM((B_HEADS, B_HEAD_DIM, B_HEAD_DIM), F32),
                        pltpu.VMEM((B_HEADS, SUBLANES, LANES), F32),
                        pltpu.VMEM((B_HEADS, SUBLANES, LANES), F32),
                        pltpu.VMEM((2, rows + SUBLANES, B_WIDTH), F32)],
        compiler_params=_cparams(2),
    )(proj, proj, proj, proj, gates, gatesT, conv_w, conv_b.reshape(1, -1), norm_g.reshape(1, -1), tril, tril.T)


def _merge_kernel(ya_ref, yb_ref, ga_ref, gb_ref, x_ref, gt_ref, wa_ref, wb_ref, wo_ref, n2_ref, sh_ref, sc_ref,
                  x1_ref, h2_ref):
    ya = jnp.dot(ya_ref[...], wa_ref[...], preferred_element_type=F32)
    yb = jnp.dot(yb_ref[...], wb_ref[...], preferred_element_type=F32)
    y = _sigmoid(ga_ref[...].astype(F32)) * ya + _sigmoid(gb_ref[...].astype(F32)) * yb
    x1 = x_ref[...] + gt_ref[...] * jnp.dot(y.astype(BF16), wo_ref[...], preferred_element_type=F32)
    x1_ref[...] = x1
    yn = x1 * lax.rsqrt(jnp.mean(x1 * x1, axis=-1, keepdims=True) + EPS) * n2_ref[...]
    h2_ref[...] = (yn * (1.0 + sc_ref[...]) + sh_ref[...]).astype(h2_ref.dtype)


def _merge(y_a, y_b, proj, x, gt1, wa, wb, wo, n2g, sh2, sc2, tm):
    B, S, D = x.shape
    row = pl.BlockSpec((None, 1, D), lambda b, i: (b, 0, 0))
    full = lambda shape: pl.BlockSpec(shape, lambda b, i: (0,) * len(shape))
    return pl.pallas_call(
        _merge_kernel, name="merge",
        grid=(B, S // tm),
        in_specs=[pl.BlockSpec((None, tm, A_WIDTH), lambda b, i: (b, i, 0)),
                  pl.BlockSpec((None, tm, B_WIDTH), lambda b, i: (b, i, 0)),
                  pl.BlockSpec((None, tm, D), lambda b, i: (b, i, PB_GA // 2)),
                  pl.BlockSpec((None, tm, D), lambda b, i: (b, i, PB_GB // 2)),
                  pl.BlockSpec((None, tm, D), lambda b, i: (b, i, 0)),
                  row, full(wa.shape), full(wb.shape), full(wo.shape), full((1, D)), row, row],
        out_specs=[pl.BlockSpec((None, tm, D), lambda b, i: (b, i, 0)),
                   pl.BlockSpec((None, tm, D), lambda b, i: (b, i, 0))],
        out_shape=[jax.ShapeDtypeStruct((B, S, D), F32), jax.ShapeDtypeStruct((B, S, D), BF16)],
        compiler_params=_cparams(2),
    )(y_a, y_b, proj, proj, x, gt1, wa, wb, wo, n2g, sh2, sc2)


def _fold_kernel(sk_ref, wq_ref, o_ref):
    o_ref[...] = lax.dot_general(sk_ref[...], wq_ref[...], NT_DIMS, preferred_element_type=F32,
                                 precision=lax.Precision.HIGHEST).astype(o_ref.dtype)


def _fold(w_pq, sub_keys):
    D = w_pq.shape[0]
    dh = D_KEY // 2
    nhp = PEER_HEADS * 2
    return pl.pallas_call(
        _fold_kernel, name="fold",
        grid=(nhp,),
        in_specs=[pl.BlockSpec((None, N_KEYS, dh), lambda j: (j, 0, 0)),
                  pl.BlockSpec((D, dh), lambda j: (0, j))],
        out_specs=pl.BlockSpec((N_KEYS, D), lambda j: (j, 0)),
        out_shape=jax.ShapeDtypeStruct((nhp * N_KEYS, D), BF16),
        compiler_params=_cparams(1),
    )(sub_keys.reshape(nhp, N_KEYS, dh), w_pq)


_CELLS = [(r1, r2) for r1 in range(PEER_TOPK) for r2 in range(PEER_TOPK) if (r1 + 1) * (r2 + 1) <= PEER_TOPK]


def _select_kernel(tt, h2_ref, ws_ref, e1_ref, n1_ref, e2_ref, r2_ref,
                   s_scr, r1_scr, e1_scr, v_scr, n_scr, z_scr):
    nlt = tt // LANES
    s = lax.dot_general(ws_ref[...], h2_ref[...], NT_DIMS, preferred_element_type=F32)
    for lt in range(nlt):
        s_scr[lt] = s[:, lt * LANES:(lt + 1) * LANES]

    def extract(it, carry):
        h = it // nlt
        lt = it % nlt
        for p in range(2):
            sv = s_scr[lt, pl.ds(pl.multiple_of((h * 2 + p) * N_KEYS, N_KEYS), N_KEYS), :]
            work = sv
            rank = jnp.full((N_KEYS, LANES), float(PEER_TOPK), F32)
            top = None
            for r in range(PEER_TOPK):
                m = jnp.max(work, axis=0, keepdims=True)
                if r == 0:
                    top = m
                eq = work == m
                rank = jnp.where(eq, float(r), rank)
                work = jnp.where(eq, NEG, work)
                v_scr[p, r, lt, pl.ds(h, 1), :] = m
            e = jnp.exp(sv - top)
            if p == 0:
                r1_scr[h, lt] = rank
                e1_scr[h, lt] = e
            else:
                r2_ref[h, lt] = rank.astype(r2_ref.dtype)
                e2_ref[h, lt] = e.astype(e2_ref.dtype)
        return carry

    lax.fori_loop(0, PEER_HEADS * nlt, extract, 0)

    def cells(lt, carry):
        v1 = [v_scr[0, r, lt] for r in range(PEER_TOPK)]
        v2 = [v_scr[1, r, lt] for r in range(PEER_TOPK)]
        vals = [v1[r1] + v2[r2] for (r1, r2) in _CELLS]
        nc = len(_CELLS)
        static = [0] * nc
        dyn = [None] * nc
        for a in range(nc):
            for b in range(a + 1, nc):
                (a1, a2), (b1, b2) = _CELLS[a], _CELLS[b]
                if a1 <= b1 and a2 <= b2:
                    static[b] += 1
                else:
                    ge = vals[a] >= vals[b]
                    inc_b = jnp.where(ge, 1.0, 0.0)
                    dyn[b] = inc_b if dyn[b] is None else dyn[b] + inc_b
                    inc_a = 1.0 - inc_b
                    dyn[a] = inc_a if dyn[a] is None else dyn[a] + inc_a
        cnt = [None] * PEER_TOPK
        z = None
        for ci, (r1, r2) in enumerate(_CELLS):
            rk = static[ci] + (dyn[ci] if dyn[ci] is not None else 0.0)
            sel = jnp.where(rk < float(PEER_TOPK), 1.0, 0.0)
            cnt[r1] = sel if cnt[r1] is None else cnt[r1] + sel
            zt = sel * jnp.exp(vals[ci] - vals[0])
            z = zt if z is None else z + zt
        for r in range(PEER_TOPK):
            n_scr[r, lt] = cnt[r]
        z_scr[lt] = 1.0 / z
        return carry

    lax.fori_loop(0, nlt, cells, 0)

    for h in range(PEER_HEADS):
        for lt in range(nlt):
            r1 = r1_scr[h, lt]
            n1 = jnp.zeros((N_KEYS, LANES), F32)
            for r in range(PEER_TOPK):
                n1 = jnp.where(r1 == float(r), n_scr[r, lt, h:h + 1, :], n1)
            e1 = e1_scr[h, lt] * z_scr[lt, h:h + 1, :]
            n1_ref[lt, pl.ds(h, N_KEYS, stride=PEER_HEADS), :] = n1
            e1_ref[lt, pl.ds(h, N_KEYS, stride=PEER_HEADS), :] = e1


def _select(h2, ws, tt):
    T, D = h2.shape
    nlt = tt // LANES
    rows = N_KEYS * PEER_HEADS
    return pl.pallas_call(
        functools.partial(_select_kernel, tt), name="select",
        grid=(T // tt,),
        in_specs=[pl.BlockSpec((tt, D), lambda i: (i, 0)),
                  pl.BlockSpec(ws.shape, lambda i: (0, 0))],
        out_specs=[pl.BlockSpec((nlt, rows, LANES), lambda i: (i, 0, 0)),
                   pl.BlockSpec((nlt, rows, LANES), lambda i: (i, 0, 0)),
                   pl.BlockSpec((PEER_HEADS, nlt, N_KEYS, LANES), lambda i: (0, i, 0, 0)),
                   pl.BlockSpec((PEER_HEADS, nlt, N_KEYS, LANES), lambda i: (0, i, 0, 0))],
        out_shape=[jax.ShapeDtypeStruct((T // LANES, rows, LANES), F32),
                   jax.ShapeDtypeStruct((T // LANES, rows, LANES), F32),
                   jax.ShapeDtypeStruct((PEER_HEADS, T // LANES, N_KEYS, LANES), BF16),
                   jax.ShapeDtypeStruct((PEER_HEADS, T // LANES, N_KEYS, LANES), BF16)],
        scratch_shapes=[pltpu.VMEM((nlt, 2 * PEER_HEADS * N_KEYS, LANES), F32),
                        pltpu.VMEM((PEER_HEADS, nlt, N_KEYS, LANES), F32),
                        pltpu.VMEM((PEER_HEADS, nlt, N_KEYS, LANES), F32),
                        pltpu.VMEM((2, PEER_TOPK, nlt, PEER_HEADS, LANES), F32),
                        pltpu.VMEM((PEER_TOPK, nlt, PEER_HEADS, LANES), F32),
                        pltpu.VMEM((nlt, PEER_HEADS, LANES), F32)],
        compiler_params=_cparams(1),
    )(h2, ws)


def _gelu_tanh_sigmoid_form(x):
    k0 = -2.0 * 0.7978845608028654 * 1.4426950408889634
    k1 = k0 * 0.044715
    return x * (1.0 / (1.0 + jnp.exp2(x * (k1 * (x * x) + k0))))


DENSE_CW = 2 * LANES
DENSE_GROUP = 2


def _dense_kernel(eb, final_norm, h2_ref, u_ref, vT_ref, e1_ref, n1_ref, e2_ref, r2_ref,
                  x1_ref, gt_ref, fg_ref, o_ref, a_scr, g_scr, acc_scr):
    j = pl.program_id(2)
    nchunk, _, cw = acc_scr.shape
    lpc = cw // LANES
    tiles = [(k, l) for k in range(eb // N_KEYS) for l in range(lpc)]

    @pl.when(j == 0)
    def _():
        acc_scr[...] = jnp.zeros_like(acc_scr)

    def group(cg, carry):
        for g in range(DENSE_GROUP):
            c = cg * DENSE_GROUP + g
            tok = pl.ds(pl.multiple_of(c * cw, cw), cw)
            a_scr[g] = lax.dot_general(u_ref[...], h2_ref[tok, :], NT_DIMS, preferred_element_type=F32)
            for k, l in tiles:
                rows = slice(k * N_KEYS, (k + 1) * N_KEYS)
                heads = slice(k * PEER_HEADS, (k + 1) * PEER_HEADS)
                lanes = slice(l * LANES, (l + 1) * LANES)
                lt = c * lpc + l
                e1_all = e1_ref[lt, heads, :]
                n1_all = n1_ref[lt, heads, :]
                gate = None
                for h in range(PEER_HEADS):
                    e1 = jnp.broadcast_to(e1_all[h:h + 1, :], (N_KEYS, LANES)).astype(BF16)
                    n1 = jnp.broadcast_to(n1_all[h:h + 1, :], (N_KEYS, LANES)).astype(BF16)
                    term = jnp.minimum(e1 * e2_ref[h, lt], jnp.maximum(n1 - r2_ref[h, lt], jnp.zeros((), BF16)))
                    gate = term if gate is None else gate + term
                g_scr[g, rows, lanes] = gate
        for g in range(DENSE_GROUP):
            c = cg * DENSE_GROUP + g
            for k, l in tiles:
                rows = slice(k * N_KEYS, (k + 1) * N_KEYS)
                lanes = slice(l * LANES, (l + 1) * LANES)
                act = _gelu_tanh_sigmoid_form(a_scr[g, rows, lanes])
                g_scr[g, rows, lanes] = g_scr[g, rows, lanes] * act.astype(BF16)
            acc_scr[c] += jnp.dot(vT_ref[...], g_scr[g], preferred_element_type=F32)
        return carry

    lax.fori_loop(0, nchunk // DENSE_GROUP, group, 0)

    @pl.when(j == pl.num_programs(2) - 1)
    def _():
        for c in range(nchunk):
            tok = slice(c * cw, (c + 1) * cw)
            x2 = x1_ref[tok, :] + gt_ref[...] * acc_scr[c].T
            if final_norm:
                x2 = x2 * lax.rsqrt(jnp.mean(x2 * x2, axis=-1, keepdims=True) + EPS) * fg_ref[...]
            o_ref[tok, :] = x2


def _dense(h2, u_bf, vT_bf, e1n, n1, e2, r2, x1, gt2, fg, final_norm, tt, eb):
    B, S, D = x1.shape
    E = u_bf.shape[0]
    nt = S // tt
    rb = eb // N_KEYS * PEER_HEADS
    cw = min(tt, DENSE_CW)
    assert (tt // cw) % DENSE_GROUP == 0
    gate_spec = pl.BlockSpec((tt // LANES, rb, LANES), lambda b, i, j: (b * nt + i, j, 0))
    tile_spec = pl.BlockSpec((PEER_HEADS, tt // LANES, N_KEYS, LANES), lambda b, i, j: (0, b * nt + i, 0, 0))
    return pl.pallas_call(
        functools.partial(_dense_kernel, eb, final_norm), name="dense",
        grid=(B, nt, E // eb),
        in_specs=[pl.BlockSpec((None, tt, D), lambda b, i, j: (b, i, 0)),
                  pl.BlockSpec((eb, D), lambda b, i, j: (j, 0)),
                  pl.BlockSpec((D, eb), lambda b, i, j: (0, j)),
                  gate_spec, gate_spec, tile_spec, tile_spec,
                  pl.BlockSpec((None, tt, D), lambda b, i, j: (b, i, 0)),
                  pl.BlockSpec((None, 1, D), lambda b, i, j: (b, 0, 0)),
                  pl.BlockSpec((1, D), lambda b, i, j: (0, 0))],
        out_specs=pl.BlockSpec((None, tt, D), lambda b, i, j: (b, i, 0)),
        out_shape=jax.ShapeDtypeStruct((B, S, D), F32),
        scratch_shapes=[pltpu.VMEM((DENSE_GROUP, eb, cw), F32), pltpu.VMEM((DENSE_GROUP, eb, cw), BF16),
                        pltpu.VMEM((tt // cw, D, cw), F32)],
        compiler_params=_cparams(3),
    )(h2, u_bf, vT_bf, e1n, n1, e2, r2, x1, gt2, fg)


def _tile(n, pref):
    return pref if n % pref == 0 else n


def kernel(x, c, w_ada, b_ada, norm1_g, w_in, conv_w, conv_b, b_igate, b_fgate, rel_bias, mlstm_norm_g,
           w_branch_a, w_branch_b, w_out, norm2_g, w_peer_q, peer_sub_keys, peer_u, peer_v, final_g):
    B, S, D = x.shape
    depth = w_ada.shape[0]
    for l in range(depth):
        mod = _mod(c, w_ada[l], b_ada[l])
        sh1, sc1, gt1, sh2, sc2, gt2 = [m.reshape(B, 1, D) for m in jnp.split(mod, 6, axis=-1)]

        o = [0]
        for sz in (A_WIDTH, A_WIDTH, A_WIDTH, 2 * B_WIDTH, B_WIDTH, B_WIDTH, B_HEADS, B_HEADS, D, D):
            o.append(o[-1] + sz)
        wl = w_in[l]
        seg = lambda i: wl[:, o[i]:o[i + 1]]
        w_main = jnp.concatenate([seg(8), seg(9), seg(0), seg(1), seg(2), seg(3), seg(4), seg(5)], axis=1).astype(BF16)
        w_gate = jnp.concatenate([seg(6), seg(7), jnp.zeros((D, LANES - 2 * B_HEADS), F32)], axis=1).astype(BF16)
        w_gateT = w_gate[:, :SUBLANES].T
        gb = jnp.concatenate([b_igate[l], b_fgate[l]]).astype(F32)
        tm = _tile(S, 1024)
        gbias = jnp.concatenate([gb, jnp.zeros((LANES - 2 * B_HEADS,), F32)]).reshape(1, LANES)
        gbiasT = jnp.broadcast_to(gb[:, None], (SUBLANES, tm))
        proj, gates, gatesT = _inproj(x, sh1, sc1, norm1_g[l].reshape(1, D), w_main, w_gate, w_gateT,
                                      gbias, gbiasT, tm, 512)

        y_a = _attention(proj, rel_bias[l], _tile(S, 256))
        y_b = _mlstm(proj, gates, gatesT, conv_w[l], conv_b[l], mlstm_norm_g[l], _tile(S, 128))
        x1, h2 = _merge(y_a, y_b, proj, x, gt1, w_branch_a[l].astype(BF16), w_branch_b[l].astype(BF16),
                        w_out[l].astype(BF16), norm2_g[l].reshape(1, D), sh2, sc2, _tile(S, 512))

        ws = _fold(w_peer_q[l], peer_sub_keys[l])
        e1n, n1, e2, r2 = _select(h2.reshape(B * S, D), ws, _tile(B * S, 512))
        x = _dense(h2, peer_u[l].astype(BF16), peer_v[l].T.astype(BF16), e1n, n1, e2, r2, x1, gt2,
                   final_g.reshape(1, D), l == depth - 1, _tile(S, 1024), 1024)
    return x
```

```python
import functools

import jax
import jax.numpy as jnp
from jax import lax
from jax.experimental import pallas as pl
from jax.experimental.pallas import tpu as pltpu

F32 = jnp.float32
BF16 = jnp.bfloat16

CHUNK = 64
LEFT_CHUNKS = 8
A_HEADS = 8
A_HEAD_DIM = 64
A_WIDTH = A_HEADS * A_HEAD_DIM
REL_CLIP = 128
B_HEADS = 4
B_HEAD_DIM = 128
B_WIDTH = B_HEADS * B_HEAD_DIM
CONV_K = 4
PEER_HEADS = 8
N_KEYS = 128
PEER_TOPK = 16
D_KEY = 256
EPS = 1e-6

LANES = 128
SUBLANES = 8
NEG = -1e30
VMEM_LIMIT = 56 * 1024 * 1024

PB_GA, PB_GB, PB_QA, PB_KA, PB_VA, PB_QP, PB_KP, PB_VB, PB_OB = 0, 2, 4, 5, 6, 7, 8, 9, 10
P_COLS = 11 * 512

NT_DIMS = (((1,), (1,)), ((), ()))
TN_DIMS = (((0,), (0,)), ((), ()))


def _cparams(n_axes, flags=None):
    return pltpu.CompilerParams(dimension_semantics=("arbitrary",) * n_axes, vmem_limit_bytes=VMEM_LIMIT,
                                flags=flags)


def _sigmoid(x):
    return 1.0 / (1.0 + jnp.exp(-x))


def _log_sigmoid(x):
    return jnp.minimum(x, 0.0) - jnp.log(1.0 + jnp.exp(-jnp.abs(x)))


def _mod_kernel(c_ref, w_ref, b_ref, o_ref):
    c = c_ref[...]
    o_ref[...] = jnp.dot(c * _sigmoid(c), w_ref[...], preferred_element_type=F32,
                         precision=lax.Precision.HIGHEST) + b_ref[...]


def _mod(c, w, b):
    B, D = c.shape
    N = w.shape[1]
    tn = 512
    return pl.pallas_call(
        _mod_kernel, name="mod",
        grid=(N // tn,),
        in_specs=[pl.BlockSpec((B, D), lambda j: (0, 0)),
                  pl.BlockSpec((D, tn), lambda j: (0, j)),
                  pl.BlockSpec((1, tn), lambda j: (0, j))],
        out_specs=pl.BlockSpec((B, tn), lambda j: (0, j)),
        out_shape=jax.ShapeDtypeStruct((B, N), F32),
        compiler_params=_cparams(1),
    )(c, w, b.reshape(1, N))


def _inproj_kernel(x_ref, sh_ref, sc_ref, g_ref, w_ref, wg_ref, wgT_ref, gb_ref, gbT_ref,
                   o_ref, og_ref, ogT_ref, h_scr):
    @pl.when(pl.program_id(2) == 0)
    def _():
        x = x_ref[...]
        y = x * lax.rsqrt(jnp.mean(x * x, axis=-1, keepdims=True) + EPS) * g_ref[...]
        hb = (y * (1.0 + sc_ref[...]) + sh_ref[...]).astype(BF16)
        h_scr[...] = hb
        og_ref[...] = jnp.dot(hb, wg_ref[...], preferred_element_type=F32) + gb_ref[...]
        ogT_ref[...] = lax.dot_general(wgT_ref[...], hb, NT_DIMS, preferred_element_type=F32) + gbT_ref[...]

    o_ref[...] = jnp.dot(h_scr[...], w_ref[...], preferred_element_type=F32).astype(o_ref.dtype)


def _inproj(x, sh, sc, g, w_main, w_gate, w_gateT, gbias, gbiasT, tm, tn):
    B, S, D = x.shape
    N = w_main.shape[1]
    row = lambda b, i, j: (b, 0, 0)
    return pl.pallas_call(
        _inproj_kernel, name="inproj",
        grid=(B, S // tm, N // tn),
        in_specs=[pl.BlockSpec((None, tm, D), lambda b, i, j: (b, i, 0)),
                  pl.BlockSpec((None, 1, D), row),
                  pl.BlockSpec((None, 1, D), row),
                  pl.BlockSpec((1, D), lambda b, i, j: (0, 0)),
                  pl.BlockSpec((D, tn), lambda b, i, j: (0, j)),
                  pl.BlockSpec((D, LANES), lambda b, i, j: (0, 0)),
                  pl.BlockSpec((SUBLANES, D), lambda b, i, j: (0, 0)),
                  pl.BlockSpec((1, LANES), lambda b, i, j: (0, 0)),
                  pl.BlockSpec((SUBLANES, tm), lambda b, i, j: (0, 0))],
        out_specs=[pl.BlockSpec((None, tm, tn), lambda b, i, j: (b, i, j)),
                   pl.BlockSpec((None, tm, LANES), lambda b, i, j: (b, i, 0)),
                   pl.BlockSpec((None, SUBLANES, tm), lambda b, i, j: (b, 0, i))],
        out_shape=[jax.ShapeDtypeStruct((B, S, N), BF16),
                   jax.ShapeDtypeStruct((B, S, LANES), F32),
                   jax.ShapeDtypeStruct((B, SUBLANES, S), F32)],
        scratch_shapes=[pltpu.VMEM((tm, D), BF16)],
        compiler_params=_cparams(3),
    )(x, sh, sc, g, w_main, w_gate, w_gateT, gbias, gbiasT)


def _attn_kernel(nprev, q_ref, *refs):
    k_refs = refs[:nprev + 1]
    v_refs = refs[nprev + 1:2 * nprev + 2]
    bias_ref, o_ref = refs[2 * nprev + 2:]
    i = pl.program_id(2)
    q = q_ref[...]
    lo = lax.broadcasted_iota(jnp.int32, (1, LANES), 1) < A_HEAD_DIM
    scale = A_HEAD_DIM ** -0.5
    heads = range(2)
    qm = [jnp.where(lo if hh == 0 else jnp.logical_not(lo), q, jnp.zeros_like(q)) for hh in heads]
    ss = [[] for _ in heads]
    for c in range(nprev + 1):
        for hh in heads:
            s = lax.dot_general(qm[hh], k_refs[c][...], NT_DIMS, preferred_element_type=F32) * scale + bias_ref[hh, c]
            if c < nprev:
                s = jnp.where(i >= nprev - c, s, NEG)
            ss[hh].append(s)
    m = [functools.reduce(jnp.maximum, [jnp.max(s, axis=-1, keepdims=True) for s in ss[hh]]) for hh in heads]
    l = [None for _ in heads]
    acc = [None for _ in heads]
    for c in range(nprev + 1):
        for hh in heads:
            p = jnp.exp(ss[hh][c] - m[hh])
            pl_ = jnp.sum(p, axis=-1, keepdims=True)
            pv = jnp.dot(p.astype(BF16), v_refs[c][...], preferred_element_type=F32)
            l[hh] = pl_ if l[hh] is None else l[hh] + pl_
            acc[hh] = pv if acc[hh] is None else acc[hh] + pv
    o_ref[...] = jnp.where(lo, acc[0] / l[0], acc[1] / l[1]).astype(o_ref.dtype)


def _attn_bias(rel_bias, tq):
    nprev = LEFT_CHUNKS * CHUNK // tq
    nk = (nprev + 1) * tq
    ql = jnp.arange(tq)[:, None]
    kl = jnp.arange(nk)[None, :] - nprev * tq
    L = nk + tq
    delta = jnp.arange(L)
    delta = jnp.where(delta < nk, delta, delta - L)
    w = rel_bias[:, jnp.clip(nprev * tq - delta, -REL_CLIP, REL_CLIP) + REL_CLIP].astype(F32)
    toep = jnp.tile(w, (1, tq))[:, :tq * (L - 1)].reshape(A_HEADS, tq, L - 1)[:, :, :nk]
    qc = ql // CHUNK
    kc = jnp.floor_divide(kl, CHUNK)
    vis = (kc <= qc) & (kc >= qc - LEFT_CHUNKS)
    bias = jnp.where(vis[None], toep, NEG)
    return bias.reshape(A_HEADS, tq, nprev + 1, tq).transpose(0, 2, 1, 3)


def _attention(proj, rel_bias, tq):
    B, S, _ = proj.shape
    nprev = LEFT_CHUNKS * CHUNK // tq
    bias = _attn_bias(rel_bias, tq)
    npair = A_HEADS // 2
    qb, kb, vb = PB_QA * 4, PB_KA * 4, PB_VA * 4

    def kv_spec(base, c):
        back = nprev - c
        return pl.BlockSpec((None, tq, LANES), lambda p, b, i: (b, jnp.maximum(i - back, 0), base + p))

    in_specs = [pl.BlockSpec((None, tq, LANES), lambda p, b, i: (b, i, qb + p))]
    in_specs += [kv_spec(kb, c) for c in range(nprev + 1)]
    in_specs += [kv_spec(vb, c) for c in range(nprev + 1)]
    in_specs += [pl.BlockSpec((2, nprev + 1, tq, tq), lambda p, b, i: (p, 0, 0, 0))]
    return pl.pallas_call(
        functools.partial(_attn_kernel, nprev), name="attn",
        grid=(npair, B, S // tq),
        in_specs=in_specs,
        out_specs=pl.BlockSpec((None, tq, LANES), lambda p, b, i: (b, i, p)),
        out_shape=jax.ShapeDtypeStruct((B, S, A_WIDTH), BF16),
        compiler_params=_cparams(3),
    )(*([proj] * (2 * nprev + 3)), bias)


MLSTM_HEAD_GROUP = 2


def _mlstm_kernel(nb, rows, qp_ref, kp_ref, v_ref, ob_ref, g_ref, gT_ref, cw_ref, cb_ref, ng_ref, tril_ref, triu_ref,
                  o_ref, C_scr, n_scr, m_scr, ext_scr):
    @pl.when(pl.program_id(1) == 0)
    def _():
        C_scr[...] = jnp.zeros_like(C_scr)
        n_scr[...] = jnp.zeros_like(n_scr)
        m_scr[...] = jnp.zeros_like(m_scr)
        ext_scr[:, :, 0:SUBLANES, :] = jnp.zeros((nb, 2, SUBLANES, B_WIDTH), F32)

    q_all, k_all = [], []
    for bb in range(nb):
        qk = []
        for t, ref in enumerate((qp_ref, kp_ref)):
            ext_scr[bb, t, SUBLANES:SUBLANES + rows, :] = ref[bb].astype(F32)
            acc = cb_ref[:, t * B_WIDTH:(t + 1) * B_WIDTH]
            for kk in range(CONV_K):
                start = SUBLANES - (CONV_K - 1) + kk
                acc = acc + cw_ref[kk:kk + 1, t * B_WIDTH:(t + 1) * B_WIDTH] * ext_scr[bb, t, start:start + rows, :]
            ext_scr[bb, t, 0:SUBLANES, :] = ext_scr[bb, t, rows:rows + SUBLANES, :]
            qk.append(acc * _sigmoid(acc))
        q_all.append(qk[0])
        k_all.append(qk[1] * (B_HEAD_DIM ** -0.5))

    causal = (lax.broadcasted_iota(jnp.int32, (CHUNK, CHUNK), 0) >= lax.broadcasted_iota(jnp.int32, (CHUNK, CHUNK), 1))
    hi = lax.Precision.HIGHEST
    each = lambda f, *cols: [f(*a) for a in zip(*cols)]

    for c in range(rows // CHUNK):
        r0 = c * CHUNK
        tsl = slice(r0, r0 + CHUNK)
        for hg in range(0, B_HEADS, MLSTM_HEAD_GROUP):
            chains = [(bb, h) for h in range(hg, hg + MLSTM_HEAD_GROUP) for bb in range(nb)]
            hsl = [slice(h * B_HEAD_DIM, (h + 1) * B_HEAD_DIM) for _, h in chains]
            slot = [bb * B_HEADS + h for bb, h in chains]
            qh = [q_all[bb][tsl, sl] for (bb, _), sl in zip(chains, hsl)]
            kh = [k_all[bb][tsl, sl] for (bb, _), sl in zip(chains, hsl)]
            vh = [v_ref[bb, tsl, sl] for (bb, _), sl in zip(chains, hsl)]
            qh_b = each(lambda x: x.astype(BF16), qh)
            li_b = [jnp.broadcast_to(g_ref[bb, tsl, h:h + 1], (CHUNK, LANES)) for bb, h in chains]
            lf_b = [_log_sigmoid(jnp.broadcast_to(g_ref[bb, tsl, B_HEADS + h:B_HEADS + h + 1], (CHUNK, LANES)))
                    for bb, h in chains]
            li_r = [jnp.broadcast_to(gT_ref[bb, h:h + 1, tsl], (CHUNK, CHUNK)) for bb, h in chains]
            lf_r = [_log_sigmoid(jnp.broadcast_to(gT_ref[bb, B_HEADS + h:B_HEADS + h + 1, tsl], (CHUNK, CHUNK)))
                    for bb, h in chains]
            a_b = each(lambda x: jnp.dot(tril_ref[...], x, preferred_element_type=F32, precision=hi), lf_b)
            a_r = each(lambda x: jnp.dot(x, triu_ref[...], preferred_element_type=F32, precision=hi), lf_r)
            D = each(lambda ab, ar, lr: jnp.where(causal, ab[:, :CHUNK] - ar + lr, NEG), a_b, a_r, li_r)

            m_prev = [m_scr[s, 0:1, :] for s in slot]
            inter = each(lambda ab, mp: ab + mp, a_b, m_prev)
            m_row = each(lambda it, d: jnp.maximum(it, jnp.max(d, axis=-1, keepdims=True)), inter, D)
            w_inter = each(lambda it, mr: jnp.exp(it - mr), inter, m_row)
            Wd = each(lambda d, mr: jnp.exp(d - mr[:, :CHUNK]), D, m_row)
            qkw = each(lambda q, k, w: lax.dot_general(q, k.astype(BF16), NT_DIMS, preferred_element_type=F32) * w,
                       qh_b, kh, Wd)
            qC = [jnp.dot(q, C_scr[s].astype(BF16), preferred_element_type=F32) for q, s in zip(qh_b, slot)]
            qkv = each(lambda p, v: jnp.dot(p.astype(BF16), v, preferred_element_type=F32), qkw, vh)
            num = each(lambda w, a, b: w * a + b, w_inter, qC, qkv)
            qn = [jnp.sum(q * n_scr[s, 0:1, :], axis=-1, keepdims=True) for q, s in zip(qh, slot)]
            den = each(lambda w, a, p: w[:, 0:1] * a + jnp.sum(p, axis=-1, keepdims=True), w_inter, qn, qkw)
            hh = each(lambda nu, de, mr: nu / jnp.maximum(jnp.abs(de), jnp.exp(-mr[:, 0:1])), num, den, m_row)

            A_row = each(lambda ab: ab[CHUNK - 1:CHUNK, :], a_b)
            g_b = each(lambda A, ab, lb: A - ab + lb, A_row, a_b, li_b)
            m_new = each(lambda A, mp, g: jnp.maximum(A + mp, jnp.max(g, axis=0, keepdims=True)), A_row, m_prev, g_b)
            decay = each(lambda A, mp, mn: jnp.exp(A + mp - mn), A_row, m_prev, m_new)
            kw = each(lambda g, mn, k: jnp.exp(g - mn) * k, g_b, m_new, kh)
            dC = each(lambda w, v: lax.dot_general(w.astype(BF16), v, TN_DIMS, preferred_element_type=F32), kw, vh)
            for s, de, d, w, mn in zip(slot, decay, dC, kw, m_new):
                C_scr[s] = de * C_scr[s] + d
                n_scr[s] = jnp.broadcast_to(de * n_scr[s, 0:1, :] + jnp.sum(w, axis=0, keepdims=True),
                                            (SUBLANES, LANES))
                m_scr[s] = jnp.broadcast_to(mn, (SUBLANES, LANES))

            for (bb, _), sl, x in zip(chains, hsl, hh):
                hn = x * lax.rsqrt(jnp.mean(x * x, axis=-1, keepdims=True) + EPS) * ng_ref[:, sl]
                ob = ob_ref[bb, tsl, sl].astype(F32)
                o_ref[bb, tsl, sl] = (_sigmoid(ob) * hn).astype(o_ref.dtype)


def _mlstm(proj, gates, gatesT, conv_w, conv_b, norm_g, rows, nb):
    B, S, _ = proj.shape
    tril = jnp.tril(jnp.ones((CHUNK, CHUNK), F32))
    blk = lambda cb: pl.BlockSpec((nb, rows, B_WIDTH), lambda b, j: (b, j, cb))
    full = lambda shape: pl.BlockSpec(shape, lambda b, j: (0,) * len(shape))
    return pl.pallas_call(
        functools.partial(_mlstm_kernel, nb, rows), name="mlstm",
        grid=(B // nb, S // rows),
        in_specs=[blk(PB_QP), blk(PB_KP), blk(PB_VB), blk(PB_OB),
                  pl.BlockSpec((nb, rows, LANES), lambda b, j: (b, j, 0)),
                  pl.BlockSpec((nb, SUBLANES, rows), lambda b, j: (b, 0, j)),
                  full((CONV_K, 2 * B_WIDTH)), full((1, 2 * B_WIDTH)), full((1, B_WIDTH)),
                  full((CHUNK, CHUNK)), full((CHUNK, CHUNK))],
        out_specs=pl.BlockSpec((nb, rows, B_WIDTH), lambda b, j: (b, j, 0)),
        out_shape=jax.ShapeDtypeStruct((B, S, B_WIDTH), BF16),
        scratch_shapes=[pltpu.VMEM((nb * B_HEADS, B_HEAD_DIM, B_HEAD_DIM), F32),
                        pltpu.VMEM((nb * B_HEADS, SUBLANES, LANES), F32),
                        pltpu.VMEM((nb * B_HEADS, SUBLANES, LANES), F32),
                        pltpu.VMEM((nb, 2, rows + SUBLANES, B_WIDTH), F32)],
        compiler_params=_cparams(2),
    )(proj, proj, proj, proj, gates, gatesT, conv_w, conv_b.reshape(1, -1), norm_g.reshape(1, -1), tril, tril.T)


def _merge_kernel(ya_ref, yb_ref, ga_ref, gb_ref, x_ref, gt_ref, wa_ref, wb_ref, wo_ref, n2_ref, sh_ref, sc_ref,
                  x1_ref, h2_ref):
    ya = jnp.dot(ya_ref[...], wa_ref[...], preferred_element_type=F32)
    yb = jnp.dot(yb_ref[...], wb_ref[...], preferred_element_type=F32)
    y = _sigmoid(ga_ref[...].astype(F32)) * ya + _sigmoid(gb_ref[...].astype(F32)) * yb
    x1 = x_ref[...] + gt_ref[...] * jnp.dot(y.astype(BF16), wo_ref[...], preferred_element_type=F32)
    x1_ref[...] = x1
    yn = x1 * lax.rsqrt(jnp.mean(x1 * x1, axis=-1, keepdims=True) + EPS) * n2_ref[...]
    h2_ref[...] = (yn * (1.0 + sc_ref[...]) + sh_ref[...]).astype(h2_ref.dtype)


def _merge(y_a, y_b, proj, x, gt1, wa, wb, wo, n2g, sh2, sc2, tm):
    B, S, D = x.shape
    row = pl.BlockSpec((None, 1, D), lambda b, i: (b, 0, 0))
    full = lambda shape: pl.BlockSpec(shape, lambda b, i: (0,) * len(shape))
    return pl.pallas_call(
        _merge_kernel, name="merge",
        grid=(B, S // tm),
        in_specs=[pl.BlockSpec((None, tm, A_WIDTH), lambda b, i: (b, i, 0)),
                  pl.BlockSpec((None, tm, B_WIDTH), lambda b, i: (b, i, 0)),
                  pl.BlockSpec((None, tm, D), lambda b, i: (b, i, PB_GA // 2)),
                  pl.BlockSpec((None, tm, D), lambda b, i: (b, i, PB_GB // 2)),
                  pl.BlockSpec((None, tm, D), lambda b, i: (b, i, 0)),
                  row, full(wa.shape), full(wb.shape), full(wo.shape), full((1, D)), row, row],
        out_specs=[pl.BlockSpec((None, tm, D), lambda b, i: (b, i, 0)),
                   pl.BlockSpec((None, tm, D), lambda b, i: (b, i, 0))],
        out_shape=[jax.ShapeDtypeStruct((B, S, D), F32), jax.ShapeDtypeStruct((B, S, D), BF16)],
        compiler_params=_cparams(2),
    )(y_a, y_b, proj, proj, x, gt1, wa, wb, wo, n2g, sh2, sc2)


def _fold_kernel(sk_ref, wq_ref, o_ref):
    o_ref[...] = lax.dot_general(sk_ref[...], wq_ref[...], NT_DIMS, preferred_element_type=F32,
                                 precision=lax.Precision.HIGHEST).astype(o_ref.dtype)


def _fold(w_pq, sub_keys):
    D = w_pq.shape[0]
    dh = D_KEY // 2
    nhp = PEER_HEADS * 2
    return pl.pallas_call(
        _fold_kernel, name="fold",
        grid=(nhp,),
        in_specs=[pl.BlockSpec((None, N_KEYS, dh), lambda j: (j, 0, 0)),
                  pl.BlockSpec((D, dh), lambda j: (0, j))],
        out_specs=pl.BlockSpec((N_KEYS, D), lambda j: (j, 0)),
        out_shape=jax.ShapeDtypeStruct((nhp * N_KEYS, D), BF16),
        compiler_params=_cparams(1),
    )(sub_keys.reshape(nhp, N_KEYS, dh), w_pq)


_CELLS = [(r1, r2) for r1 in range(PEER_TOPK) for r2 in range(PEER_TOPK) if (r1 + 1) * (r2 + 1) <= PEER_TOPK]


def _batcher_pairs(lo, hi):
    def merge(lo, hi, r):
        step = r * 2
        if step < hi - lo:
            yield from merge(lo, hi, step)
            yield from merge(lo + r, hi, step)
            yield from [(i, i + r) for i in range(lo + r, hi - r, step)]
        else:
            yield (lo, lo + r)

    if hi - lo >= 1:
        mid = lo + (hi - lo) // 2
        yield from _batcher_pairs(lo, mid)
        yield from _batcher_pairs(mid + 1, hi)
        yield from merge(lo, hi, 1)


_KEY_GROUPS = N_KEYS // SUBLANES
_SORT_NET = list(_batcher_pairs(0, _KEY_GROUPS - 1))


def _select_kernel(tt, h2_ref, ws_ref, e1_ref, n1_ref, e2_ref, r2_ref,
                   s_scr, e1_scr, v_scr, n_scr, z_scr):
    nlt = tt // LANES
    s = lax.dot_general(ws_ref[...], h2_ref[...], NT_DIMS, preferred_element_type=F32)
    for lt in range(nlt):
        s_scr[lt] = s[:, lt * LANES:(lt + 1) * LANES]

    def extract(it, carry):
        h = it // nlt
        lt = it % nlt
        for p in range(2):
            sv = s_scr[lt, pl.ds(pl.multiple_of((h * 2 + p) * N_KEYS, N_KEYS), N_KEYS), :]
            cols = [sv[i * SUBLANES:(i + 1) * SUBLANES, :] for i in range(_KEY_GROUPS)]
            for i, j in _SORT_NET:
                cols[i], cols[j] = jnp.maximum(cols[i], cols[j]), jnp.minimum(cols[i], cols[j])
            tops = []
            for r in range(PEER_TOPK):
                m = jnp.max(cols[0], axis=0, keepdims=True)
                tops.append(m)
                v_scr[p, r, lt, pl.ds(h, 1), :] = m
                live = PEER_TOPK - 1 - r
                if live > 0:
                    eq = cols[0] == m
                    for i in range(live):
                        cols[i] = jnp.where(eq, cols[i + 1], cols[i])
            e = jnp.exp(sv - tops[0])
            if p == 0:
                e1_scr[h, lt] = e
            else:
                rank = jnp.full((N_KEYS, LANES), float(PEER_TOPK), F32)
                for r in reversed(range(PEER_TOPK)):
                    rank = jnp.where(sv >= tops[r], float(r), rank)
                r2_ref[h, lt] = rank.astype(r2_ref.dtype)
                e2_ref[h, lt] = e.astype(e2_ref.dtype)
        return carry

    lax.fori_loop(0, PEER_HEADS * nlt, extract, 0)

    def cells(lt, carry):
        v1 = [v_scr[0, r, lt] for r in range(PEER_TOPK)]
        v2 = [v_scr[1, r, lt] for r in range(PEER_TOPK)]
        vals = [v1[r1] + v2[r2] for (r1, r2) in _CELLS]
        nc = len(_CELLS)
        static = [0] * nc
        dyn = [None] * nc
        for a in range(nc):
            for b in range(a + 1, nc):
                (a1, a2), (b1, b2) = _CELLS[a], _CELLS[b]
                if a1 <= b1 and a2 <= b2:
                    static[b] += 1
                else:
                    ge = vals[a] >= vals[b]
                    inc_b = jnp.where(ge, 1.0, 0.0)
                    dyn[b] = inc_b if dyn[b] is None else dyn[b] + inc_b
                    inc_a = 1.0 - inc_b
                    dyn[a] = inc_a if dyn[a] is None else dyn[a] + inc_a
        cnt = [None] * PEER_TOPK
        z = None
        for ci, (r1, r2) in enumerate(_CELLS):
            rk = static[ci] + (dyn[ci] if dyn[ci] is not None else 0.0)
            sel = jnp.where(rk < float(PEER_TOPK), 1.0, 0.0)
            cnt[r1] = sel if cnt[r1] is None else cnt[r1] + sel
            zt = sel * jnp.exp(vals[ci] - vals[0])
            z = zt if z is None else z + zt
        for r in range(PEER_TOPK):
            n_scr[r, lt] = cnt[r]
        z_scr[lt] = 1.0 / z
        return carry

    lax.fori_loop(0, nlt, cells, 0)

    for h in range(PEER_HEADS):
        for lt in range(nlt):
            s1 = s_scr[lt, 2 * h * N_KEYS:(2 * h + 1) * N_KEYS, :]
            n1 = jnp.zeros((N_KEYS, LANES), F32)
            for r in reversed(range(PEER_TOPK)):
                n1 = jnp.where(s1 >= v_scr[0, r, lt, h:h + 1, :], n_scr[r, lt, h:h + 1, :], n1)
            e1 = e1_scr[h, lt] * z_scr[lt, h:h + 1, :]
            n1_ref[lt, pl.ds(h, N_KEYS, stride=PEER_HEADS), :] = n1
            e1_ref[lt, pl.ds(h, N_KEYS, stride=PEER_HEADS), :] = e1


def _select(h2, ws, tt):
    T, D = h2.shape
    nlt = tt // LANES
    rows = N_KEYS * PEER_HEADS
    return pl.pallas_call(
        functools.partial(_select_kernel, tt), name="select",
        grid=(T // tt,),
        in_specs=[pl.BlockSpec((tt, D), lambda i: (i, 0)),
                  pl.BlockSpec(ws.shape, lambda i: (0, 0))],
        out_specs=[pl.BlockSpec((nlt, rows, LANES), lambda i: (i, 0, 0)),
                   pl.BlockSpec((nlt, rows, LANES), lambda i: (i, 0, 0)),
                   pl.BlockSpec((PEER_HEADS, nlt, N_KEYS, LANES), lambda i: (0, i, 0, 0)),
                   pl.BlockSpec((PEER_HEADS, nlt, N_KEYS, LANES), lambda i: (0, i, 0, 0))],
        out_shape=[jax.ShapeDtypeStruct((T // LANES, rows, LANES), F32),
                   jax.ShapeDtypeStruct((T // LANES, rows, LANES), F32),
                   jax.ShapeDtypeStruct((PEER_HEADS, T // LANES, N_KEYS, LANES), BF16),
                   jax.ShapeDtypeStruct((PEER_HEADS, T // LANES, N_KEYS, LANES), BF16)],
        scratch_shapes=[pltpu.VMEM((nlt, 2 * PEER_HEADS * N_KEYS, LANES), F32),
                        pltpu.VMEM((PEER_HEADS, nlt, N_KEYS, LANES), F32),
                        pltpu.VMEM((2, PEER_TOPK, nlt, PEER_HEADS, LANES), F32),
                        pltpu.VMEM((PEER_TOPK, nlt, PEER_HEADS, LANES), F32),
                        pltpu.VMEM((nlt, PEER_HEADS, LANES), F32)],
        compiler_params=_cparams(1),
    )(h2, ws)


def _gelu_tanh_sigmoid_form(x):
    k0 = -2.0 * 0.7978845608028654 * 1.4426950408889634
    k1 = k0 * 0.044715
    return x * (1.0 / (1.0 + jnp.exp2(x * (k1 * (x * x) + k0))))


DENSE_CW = 2 * LANES
DENSE_RS = 2 * LANES


def _dense_kernel(eb, final_norm, h2_ref, u_ref, vT_ref, e1_ref, n1_ref, e2_ref, r2_ref,
                  x1_ref, gt_ref, fg_ref, o_ref, a_scr, g_scr, acc_scr):
    j = pl.program_id(2)
    nchunk, d_model, cw = acc_scr.shape
    lpc = cw // LANES
    rs = DENSE_RS
    nsl = eb // rs
    assert d_model // rs == nsl

    @pl.when(j == 0)
    def _():
        acc_scr[...] = jnp.zeros_like(acc_scr)

    def slice_tiles(i):
        return [(k, l) for k in range(i * rs // N_KEYS, (i + 1) * rs // N_KEYS) for l in range(lpc)]

    def score_slice(c, i):
        a = lax.dot_general(u_ref[i * rs:(i + 1) * rs, :], h2_ref[c * cw:(c + 1) * cw, :], NT_DIMS,
                            preferred_element_type=F32)
        a_scr[c % 2, i * rs:(i + 1) * rs, :] = _gelu_tanh_sigmoid_form(a).astype(BF16)

    def weight_slice(c, i):
        for k, l in slice_tiles(i):
            rows = slice(k * N_KEYS, (k + 1) * N_KEYS)
            heads = slice(k * PEER_HEADS, (k + 1) * PEER_HEADS)
            lanes = slice(l * LANES, (l + 1) * LANES)
            lt = c * lpc + l
            e1_all = e1_ref[lt, heads, :]
            n1_all = n1_ref[lt, heads, :]
            gate = None
            for h in range(PEER_HEADS):
                e1 = jnp.broadcast_to(e1_all[h:h + 1, :], (N_KEYS, LANES)).astype(BF16)
                n1 = jnp.broadcast_to(n1_all[h:h + 1, :], (N_KEYS, LANES)).astype(BF16)
                term = jnp.minimum(e1 * e2_ref[h, lt], jnp.maximum(n1 - r2_ref[h, lt], jnp.zeros((), BF16)))
                gate = term if gate is None else gate + term
            g_scr[c % 2, rows, lanes] = gate * a_scr[c % 2, rows, lanes]

    def output_slice(c, i):
        acc_scr[c, i * rs:(i + 1) * rs, :] += jnp.dot(vT_ref[i * rs:(i + 1) * rs, :], g_scr[c % 2],
                                                      preferred_element_type=F32)

    for c in range(nchunk + 1):
        for i in range(nsl):
            if c < nchunk:
                score_slice(c, i)
            if c > 0:
                output_slice(c - 1, i)
            if c < nchunk and i > 0:
                weight_slice(c, i - 1)
        if c < nchunk:
            weight_slice(c, nsl - 1)

    @pl.when(j == pl.num_programs(2) - 1)
    def _():
        for c in range(nchunk):
            tok = slice(c * cw, (c + 1) * cw)
            x2 = x1_ref[tok, :] + gt_ref[...] * acc_scr[c].T
            if final_norm:
                x2 = x2 * lax.rsqrt(jnp.mean(x2 * x2, axis=-1, keepdims=True) + EPS) * fg_ref[...]
            o_ref[tok, :] = x2


def _dense(h2, u_bf, vT_bf, e1n, n1, e2, r2, x1, gt2, fg, final_norm, tt, eb):
    B, S, D = x1.shape
    E = u_bf.shape[0]
    nt = S // tt
    rb = eb // N_KEYS * PEER_HEADS
    cw = min(tt, DENSE_CW)
    assert eb == D and eb % DENSE_RS == 0
    gate_spec = pl.BlockSpec((tt // LANES, rb, LANES), lambda b, i, j: (b * nt + i, j, 0))
    tile_spec = pl.BlockSpec((PEER_HEADS, tt // LANES, N_KEYS, LANES), lambda b, i, j: (0, b * nt + i, 0, 0))
    return pl.pallas_call(
        functools.partial(_dense_kernel, eb, final_norm), name="dense",
        grid=(B, nt, E // eb),
        in_specs=[pl.BlockSpec((None, tt, D), lambda b, i, j: (b, i, 0)),
                  pl.BlockSpec((eb, D), lambda b, i, j: (j, 0)),
                  pl.BlockSpec((D, eb), lambda b, i, j: (0, j)),
                  gate_spec, gate_spec, tile_spec, tile_spec,
                  pl.BlockSpec((None, tt, D), lambda b, i, j: (b, i, 0)),
                  pl.BlockSpec((None, 1, D), lambda b, i, j: (b, 0, 0)),
                  pl.BlockSpec((1, D), lambda b, i, j: (0, 0))],
        out_specs=pl.BlockSpec((None, tt, D), lambda b, i, j: (b, i, 0)),
        out_shape=jax.ShapeDtypeStruct((B, S, D), F32),
        scratch_shapes=[pltpu.VMEM((2, eb, cw), BF16), pltpu.VMEM((2, eb, cw), BF16),
                        pltpu.VMEM((tt // cw, D, cw), F32)],
        compiler_params=_cparams(3),
    )(h2, u_bf, vT_bf, e1n, n1, e2, r2, x1, gt2, fg)


def _tile(n, pref):
    return pref if n % pref == 0 else n


def kernel(x, c, w_ada, b_ada, norm1_g, w_in, conv_w, conv_b, b_igate, b_fgate, rel_bias, mlstm_norm_g,
           w_branch_a, w_branch_b, w_out, norm2_g, w_peer_q, peer_sub_keys, peer_u, peer_v, final_g):
    B, S, D = x.shape
    depth = w_ada.shape[0]
    for l in range(depth):
        mod = _mod(c, w_ada[l], b_ada[l])
        sh1, sc1, gt1, sh2, sc2, gt2 = [m.reshape(B, 1, D) for m in jnp.split(mod, 6, axis=-1)]

        o = [0]
        for sz in (A_WIDTH, A_WIDTH, A_WIDTH, 2 * B_WIDTH, B_WIDTH, B_WIDTH, B_HEADS, B_HEADS, D, D):
            o.append(o[-1] + sz)
        wl = w_in[l]
        seg = lambda i: wl[:, o[i]:o[i + 1]]
        w_main = jnp.concatenate([seg(8), seg(9), seg(0), seg(1), seg(2), seg(3), seg(4), seg(5)], axis=1).astype(BF16)
        w_gate = jnp.concatenate([seg(6), seg(7), jnp.zeros((D, LANES - 2 * B_HEADS), F32)], axis=1).astype(BF16)
        w_gateT = w_gate[:, :SUBLANES].T
        gb = jnp.concatenate([b_igate[l], b_fgate[l]]).astype(F32)
        tm = _tile(S, 512)
        gbias = jnp.concatenate([gb, jnp.zeros((LANES - 2 * B_HEADS,), F32)]).reshape(1, LANES)
        gbiasT = jnp.broadcast_to(gb[:, None], (SUBLANES, tm))
        proj, gates, gatesT = _inproj(x, sh1, sc1, norm1_g[l].reshape(1, D), w_main, w_gate, w_gateT,
                                      gbias, gbiasT, tm, P_COLS)

        y_a = _attention(proj, rel_bias[l], _tile(S, 256))
        y_b = _mlstm(proj, gates, gatesT, conv_w[l], conv_b[l], mlstm_norm_g[l], _tile(S, 128), _tile(B, 2))
        x1, h2 = _merge(y_a, y_b, proj, x, gt1, w_branch_a[l].astype(BF16), w_branch_b[l].astype(BF16),
                        w_out[l].astype(BF16), norm2_g[l].reshape(1, D), sh2, sc2, _tile(S, 512))

        ws = _fold(w_peer_q[l], peer_sub_keys[l])
        e1n, n1, e2, r2 = _select(h2.reshape(B * S, D), ws, _tile(B * S, 512))
        x = _dense(h2, peer_u[l].astype(BF16), peer_v[l].T.astype(BF16), e1n, n1, e2, r2, x1, gt2,
                   final_g.reshape(1, D), l == depth - 1, _tile(S, 1024), 1024)
    return x
```

```python
import functools

import jax
import jax.numpy as jnp
from jax import lax
from jax.experimental import pallas as pl
from jax.experimental.pallas import tpu as pltpu

F32 = jnp.float32
BF16 = jnp.bfloat16

CHUNK = 64
LEFT_CHUNKS = 8
A_HEADS = 8
A_HEAD_DIM = 64
A_WIDTH = A_HEADS * A_HEAD_DIM
REL_CLIP = 128
B_HEADS = 4
B_HEAD_DIM = 128
B_WIDTH = B_HEADS * B_HEAD_DIM
CONV_K = 4
PEER_HEADS = 8
N_KEYS = 128
PEER_TOPK = 16
D_KEY = 256
EPS = 1e-6

LANES = 128
SUBLANES = 8
NEG = -1e30
VMEM_LIMIT = 56 * 1024 * 1024

PB_GA, PB_GB, PB_QA, PB_KA, PB_VA, PB_QP, PB_KP, PB_VB, PB_OB = 0, 2, 4, 5, 6, 7, 8, 9, 10
P_COLS = 11 * 512

NT_DIMS = (((1,), (1,)), ((), ()))
TN_DIMS = (((0,), (0,)), ((), ()))


def _cparams(n_axes, flags=None):
    return pltpu.CompilerParams(dimension_semantics=("arbitrary",) * n_axes, vmem_limit_bytes=VMEM_LIMIT,
                                flags=flags)


def _sigmoid(x):
    return 1.0 / (1.0 + jnp.exp(-x))


def _log_sigmoid(x):
    return jnp.minimum(x, 0.0) - jnp.log(1.0 + jnp.exp(-jnp.abs(x)))


def _mod_kernel(c_ref, w_ref, b_ref, o_ref):
    c = c_ref[...]
    o_ref[...] = jnp.dot(c * _sigmoid(c), w_ref[...], preferred_element_type=F32,
                         precision=lax.Precision.HIGHEST) + b_ref[...]


def _mod(c, w, b):
    B, D = c.shape
    N = w.shape[1]
    tn = 512
    return pl.pallas_call(
        _mod_kernel, name="mod",
        grid=(N // tn,),
        in_specs=[pl.BlockSpec((B, D), lambda j: (0, 0)),
                  pl.BlockSpec((D, tn), lambda j: (0, j)),
                  pl.BlockSpec((1, tn), lambda j: (0, j))],
        out_specs=pl.BlockSpec((B, tn), lambda j: (0, j)),
        out_shape=jax.ShapeDtypeStruct((B, N), F32),
        compiler_params=_cparams(1),
    )(c, w, b.reshape(1, N))


def _inproj_kernel(x_ref, sh_ref, sc_ref, g_ref, w_ref, wg_ref, wgT_ref, gb_ref, gbT_ref,
                   o_ref, og_ref, ogT_ref, h_scr):
    @pl.when(pl.program_id(2) == 0)
    def _():
        x = x_ref[...]
        y = x * lax.rsqrt(jnp.mean(x * x, axis=-1, keepdims=True) + EPS) * g_ref[...]
        hb = (y * (1.0 + sc_ref[...]) + sh_ref[...]).astype(BF16)
        h_scr[...] = hb
        og_ref[...] = jnp.dot(hb, wg_ref[...], preferred_element_type=F32) + gb_ref[...]
        ogT_ref[...] = lax.dot_general(wgT_ref[...], hb, NT_DIMS, preferred_element_type=F32) + gbT_ref[...]

    o_ref[...] = jnp.dot(h_scr[...], w_ref[...], preferred_element_type=F32).astype(o_ref.dtype)


def _inproj(x, sh, sc, g, w_main, w_gate, w_gateT, gbias, gbiasT, tm, tn):
    B, S, D = x.shape
    N = w_main.shape[1]
    row = lambda b, i, j: (b, 0, 0)
    return pl.pallas_call(
        _inproj_kernel, name="inproj",
        grid=(B, S // tm, N // tn),
        in_specs=[pl.BlockSpec((None, tm, D), lambda b, i, j: (b, i, 0)),
                  pl.BlockSpec((None, 1, D), row),
                  pl.BlockSpec((None, 1, D), row),
                  pl.BlockSpec((1, D), lambda b, i, j: (0, 0)),
                  pl.BlockSpec((D, tn), lambda b, i, j: (0, j)),
                  pl.BlockSpec((D, LANES), lambda b, i, j: (0, 0)),
                  pl.BlockSpec((SUBLANES, D), lambda b, i, j: (0, 0)),
                  pl.BlockSpec((1, LANES), lambda b, i, j: (0, 0)),
                  pl.BlockSpec((SUBLANES, tm), lambda b, i, j: (0, 0))],
        out_specs=[pl.BlockSpec((None, tm, tn), lambda b, i, j: (b, i, j)),
                   pl.BlockSpec((None, tm, LANES), lambda b, i, j: (b, i, 0)),
                   pl.BlockSpec((None, SUBLANES, tm), lambda b, i, j: (b, 0, i))],
        out_shape=[jax.ShapeDtypeStruct((B, S, N), BF16),
                   jax.ShapeDtypeStruct((B, S, LANES), F32),
                   jax.ShapeDtypeStruct((B, SUBLANES, S), F32)],
        scratch_shapes=[pltpu.VMEM((tm, D), BF16)],
        compiler_params=_cparams(3),
    )(x, sh, sc, g, w_main, w_gate, w_gateT, gbias, gbiasT)


def _attn_kernel(nprev, nq, q_ref, *refs):
    nkv = nprev + nq
    k_refs = refs[:nkv]
    v_refs = refs[nkv:2 * nkv]
    bias_ref, o_ref = refs[2 * nkv:]
    tq = q_ref.shape[0] // nq
    i = pl.program_id(2)
    lo = lax.broadcasted_iota(jnp.int32, (1, LANES), 1) < A_HEAD_DIM
    scale = A_HEAD_DIM ** -0.5
    probs = [(qq, hh) for qq in range(nq) for hh in range(2)]
    qm = []
    for qq, hh in probs:
        q = q_ref[qq * tq:(qq + 1) * tq, :]
        qm.append(jnp.where(lo if hh == 0 else jnp.logical_not(lo), q, jnp.zeros_like(q)))
    ss = [[] for _ in probs]
    for c in range(nprev + 1):
        for n, (qq, hh) in enumerate(probs):
            s = lax.dot_general(qm[n], k_refs[qq + c][...], NT_DIMS, preferred_element_type=F32) * scale
            s = s + bias_ref[hh, c]
            if c < nprev:
                s = jnp.where(i * nq + qq >= nprev - c, s, NEG)
            ss[n].append(s)
    m = [functools.reduce(jnp.maximum, [jnp.max(s, axis=-1, keepdims=True) for s in ss[n]]) for n in range(len(probs))]
    l = [None for _ in probs]
    acc = [None for _ in probs]
    for c in range(nprev + 1):
        for n, (qq, hh) in enumerate(probs):
            p = jnp.exp(ss[n][c] - m[n])
            pl_ = jnp.sum(p, axis=-1, keepdims=True)
            pv = jnp.dot(p.astype(BF16), v_refs[qq + c][...], preferred_element_type=F32)
            l[n] = pl_ if l[n] is None else l[n] + pl_
            acc[n] = pv if acc[n] is None else acc[n] + pv
    for qq in range(nq):
        a0, a1 = 2 * qq, 2 * qq + 1
        o_ref[qq * tq:(qq + 1) * tq, :] = jnp.where(lo, acc[a0] / l[a0], acc[a1] / l[a1]).astype(o_ref.dtype)


def _attn_bias(rel_bias, tq):
    nprev = LEFT_CHUNKS * CHUNK // tq
    nk = (nprev + 1) * tq
    ql = jnp.arange(tq)[:, None]
    kl = jnp.arange(nk)[None, :] - nprev * tq
    L = nk + tq
    delta = jnp.arange(L)
    delta = jnp.where(delta < nk, delta, delta - L)
    w = rel_bias[:, jnp.clip(nprev * tq - delta, -REL_CLIP, REL_CLIP) + REL_CLIP].astype(F32)
    toep = jnp.tile(w, (1, tq))[:, :tq * (L - 1)].reshape(A_HEADS, tq, L - 1)[:, :, :nk]
    qc = ql // CHUNK
    kc = jnp.floor_divide(kl, CHUNK)
    vis = (kc <= qc) & (kc >= qc - LEFT_CHUNKS)
    bias = jnp.where(vis[None], toep, NEG)
    return bias.reshape(A_HEADS, tq, nprev + 1, tq).transpose(0, 2, 1, 3)


def _attention(proj, rel_bias, tq, nq):
    B, S, _ = proj.shape
    nprev = LEFT_CHUNKS * CHUNK // tq
    nkv = nprev + nq
    bias = _attn_bias(rel_bias, tq)
    npair = A_HEADS // 2
    qb, kb, vb = PB_QA * 4, PB_KA * 4, PB_VA * 4

    def kv_spec(base, c):
        back = nprev - c
        return pl.BlockSpec((None, tq, LANES), lambda p, b, i: (b, jnp.maximum(i * nq - back, 0), base + p))

    in_specs = [pl.BlockSpec((None, nq * tq, LANES), lambda p, b, i: (b, i, qb + p))]
    in_specs += [kv_spec(kb, c) for c in range(nkv)]
    in_specs += [kv_spec(vb, c) for c in range(nkv)]
    in_specs += [pl.BlockSpec((2, nprev + 1, tq, tq), lambda p, b, i: (p, 0, 0, 0))]
    return pl.pallas_call(
        functools.partial(_attn_kernel, nprev, nq), name="attn",
        grid=(npair, B, S // (nq * tq)),
        in_specs=in_specs,
        out_specs=pl.BlockSpec((None, nq * tq, LANES), lambda p, b, i: (b, i, p)),
        out_shape=jax.ShapeDtypeStruct((B, S, A_WIDTH), BF16),
        compiler_params=_cparams(3),
    )(*([proj] * (2 * nkv + 1)), bias)


MLSTM_HEAD_GROUP = 2


def _mlstm_kernel(nb, rows, qp_ref, kp_ref, v_ref, ob_ref, g_ref, gT_ref, cw_ref, cb_ref, ng_ref, tril_ref, triu_ref,
                  o_ref, C_scr, n_scr, m_scr, ext_scr):
    @pl.when(pl.program_id(1) == 0)
    def _():
        C_scr[...] = jnp.zeros_like(C_scr)
        n_scr[...] = jnp.zeros_like(n_scr)
        m_scr[...] = jnp.zeros_like(m_scr)
        ext_scr[:, :, 0:SUBLANES, :] = jnp.zeros((nb, 2, SUBLANES, B_WIDTH), F32)

    q_all, k_all = [], []
    for bb in range(nb):
        qk = []
        for t, ref in enumerate((qp_ref, kp_ref)):
            ext_scr[bb, t, SUBLANES:SUBLANES + rows, :] = ref[bb].astype(F32)
            acc = cb_ref[:, t * B_WIDTH:(t + 1) * B_WIDTH]
            for kk in range(CONV_K):
                start = SUBLANES - (CONV_K - 1) + kk
                acc = acc + cw_ref[kk:kk + 1, t * B_WIDTH:(t + 1) * B_WIDTH] * ext_scr[bb, t, start:start + rows, :]
            ext_scr[bb, t, 0:SUBLANES, :] = ext_scr[bb, t, rows:rows + SUBLANES, :]
            qk.append(acc * _sigmoid(acc))
        q_all.append(qk[0])
        k_all.append(qk[1] * (B_HEAD_DIM ** -0.5))

    causal = (lax.broadcasted_iota(jnp.int32, (CHUNK, CHUNK), 0) >= lax.broadcasted_iota(jnp.int32, (CHUNK, CHUNK), 1))
    hi = lax.Precision.HIGHEST
    each = lambda f, *cols: [f(*a) for a in zip(*cols)]

    for c in range(rows // CHUNK):
        r0 = c * CHUNK
        tsl = slice(r0, r0 + CHUNK)
        for hg in range(0, B_HEADS, MLSTM_HEAD_GROUP):
            chains = [(bb, h) for h in range(hg, hg + MLSTM_HEAD_GROUP) for bb in range(nb)]
            hsl = [slice(h * B_HEAD_DIM, (h + 1) * B_HEAD_DIM) for _, h in chains]
            slot = [bb * B_HEADS + h for bb, h in chains]
            qh = [q_all[bb][tsl, sl] for (bb, _), sl in zip(chains, hsl)]
            kh = [k_all[bb][tsl, sl] for (bb, _), sl in zip(chains, hsl)]
            vh = [v_ref[bb, tsl, sl] for (bb, _), sl in zip(chains, hsl)]
            qh_b = each(lambda x: x.astype(BF16), qh)
            li_b = [jnp.broadcast_to(g_ref[bb, tsl, h:h + 1], (CHUNK, LANES)) for bb, h in chains]
            lf_b = [_log_sigmoid(jnp.broadcast_to(g_ref[bb, tsl, B_HEADS + h:B_HEADS + h + 1], (CHUNK, LANES)))
                    for bb, h in chains]
            li_r = [jnp.broadcast_to(gT_ref[bb, h:h + 1, tsl], (CHUNK, CHUNK)) for bb, h in chains]
            lf_r = [_log_sigmoid(jnp.broadcast_to(gT_ref[bb, B_HEADS + h:B_HEADS + h + 1, tsl], (CHUNK, CHUNK)))
                    for bb, h in chains]
            a_b = each(lambda x: jnp.dot(tril_ref[...], x, preferred_element_type=F32, precision=hi), lf_b)
            a_r = each(lambda x: jnp.dot(x, triu_ref[...], preferred_element_type=F32, precision=hi), lf_r)
            D = each(lambda ab, ar, lr: jnp.where(causal, ab[:, :CHUNK] - ar + lr, NEG), a_b, a_r, li_r)

            m_prev = [m_scr[s, 0:1, :] for s in slot]
            inter = each(lambda ab, mp: ab + mp, a_b, m_prev)
            m_row = each(lambda it, d: jnp.maximum(it, jnp.max(d, axis=-1, keepdims=True)), inter, D)
            w_inter = each(lambda it, mr: jnp.exp(it - mr), inter, m_row)
            Wd = each(lambda d, mr: jnp.exp(d - mr[:, :CHUNK]), D, m_row)
            qkw = each(lambda q, k, w: lax.dot_general(q, k.astype(BF16), NT_DIMS, preferred_element_type=F32) * w,
                       qh_b, kh, Wd)
            qC = [jnp.dot(q, C_scr[s].astype(BF16), preferred_element_type=F32) for q, s in zip(qh_b, slot)]
            qkv = each(lambda p, v: jnp.dot(p.astype(BF16), v, preferred_element_type=F32), qkw, vh)
            num = each(lambda w, a, b: w * a + b, w_inter, qC, qkv)
            qn = [jnp.sum(q * n_scr[s, 0:1, :], axis=-1, keepdims=True) for q, s in zip(qh, slot)]
            den = each(lambda w, a, p: w[:, 0:1] * a + jnp.sum(p, axis=-1, keepdims=True), w_inter, qn, qkw)
            hh = each(lambda nu, de, mr: nu / jnp.maximum(jnp.abs(de), jnp.exp(-mr[:, 0:1])), num, den, m_row)

            A_row = each(lambda ab: ab[CHUNK - 1:CHUNK, :], a_b)
            g_b = each(lambda A, ab, lb: A - ab + lb, A_row, a_b, li_b)
            m_new = each(lambda A, mp, g: jnp.maximum(A + mp, jnp.max(g, axis=0, keepdims=True)), A_row, m_prev, g_b)
            decay = each(lambda A, mp, mn: jnp.exp(A + mp - mn), A_row, m_prev, m_new)
            kw = each(lambda g, mn, k: jnp.exp(g - mn) * k, g_b, m_new, kh)
            dC = each(lambda w, v: lax.dot_general(w.astype(BF16), v, TN_DIMS, preferred_element_type=F32), kw, vh)
            for s, de, d, w, mn in zip(slot, decay, dC, kw, m_new):
                C_scr[s] = de * C_scr[s] + d
                n_scr[s] = jnp.broadcast_to(de * n_scr[s, 0:1, :] + jnp.sum(w, axis=0, keepdims=True),
                                            (SUBLANES, LANES))
                m_scr[s] = jnp.broadcast_to(mn, (SUBLANES, LANES))

            for (bb, _), sl, x in zip(chains, hsl, hh):
                hn = x * lax.rsqrt(jnp.mean(x * x, axis=-1, keepdims=True) + EPS) * ng_ref[:, sl]
                ob = ob_ref[bb, tsl, sl].astype(F32)
                o_ref[bb, tsl, sl] = (_sigmoid(ob) * hn).astype(o_ref.dtype)


def _mlstm(proj, gates, gatesT, conv_w, conv_b, norm_g, rows, nb):
    B, S, _ = proj.shape
    tril = jnp.tril(jnp.ones((CHUNK, CHUNK), F32))
    blk = lambda cb: pl.BlockSpec((nb, rows, B_WIDTH), lambda b, j: (b, j, cb))
    full = lambda shape: pl.BlockSpec(shape, lambda b, j: (0,) * len(shape))
    return pl.pallas_call(
        functools.partial(_mlstm_kernel, nb, rows), name="mlstm",
        grid=(B // nb, S // rows),
        in_specs=[blk(PB_QP), blk(PB_KP), blk(PB_VB), blk(PB_OB),
                  pl.BlockSpec((nb, rows, LANES), lambda b, j: (b, j, 0)),
                  pl.BlockSpec((nb, SUBLANES, rows), lambda b, j: (b, 0, j)),
                  full((CONV_K, 2 * B_WIDTH)), full((1, 2 * B_WIDTH)), full((1, B_WIDTH)),
                  full((CHUNK, CHUNK)), full((CHUNK, CHUNK))],
        out_specs=pl.BlockSpec((nb, rows, B_WIDTH), lambda b, j: (b, j, 0)),
        out_shape=jax.ShapeDtypeStruct((B, S, B_WIDTH), BF16),
        scratch_shapes=[pltpu.VMEM((nb * B_HEADS, B_HEAD_DIM, B_HEAD_DIM), F32),
                        pltpu.VMEM((nb * B_HEADS, SUBLANES, LANES), F32),
                        pltpu.VMEM((nb * B_HEADS, SUBLANES, LANES), F32),
                        pltpu.VMEM((nb, 2, rows + SUBLANES, B_WIDTH), F32)],
        compiler_params=_cparams(2),
    )(proj, proj, proj, proj, gates, gatesT, conv_w, conv_b.reshape(1, -1), norm_g.reshape(1, -1), tril, tril.T)


def _merge_kernel(ya_ref, yb_ref, ga_ref, gb_ref, x_ref, gt_ref, wa_ref, wb_ref, wo_ref, n2_ref, sh_ref, sc_ref,
                  x1_ref, h2_ref):
    ya = jnp.dot(ya_ref[...], wa_ref[...], preferred_element_type=F32)
    yb = jnp.dot(yb_ref[...], wb_ref[...], preferred_element_type=F32)
    y = _sigmoid(ga_ref[...].astype(F32)) * ya + _sigmoid(gb_ref[...].astype(F32)) * yb
    x1 = x_ref[...] + gt_ref[...] * jnp.dot(y.astype(BF16), wo_ref[...], preferred_element_type=F32)
    x1_ref[...] = x1
    yn = x1 * lax.rsqrt(jnp.mean(x1 * x1, axis=-1, keepdims=True) + EPS) * n2_ref[...]
    h2_ref[...] = (yn * (1.0 + sc_ref[...]) + sh_ref[...]).astype(h2_ref.dtype)


def _merge(y_a, y_b, proj, x, gt1, wa, wb, wo, n2g, sh2, sc2, tm):
    B, S, D = x.shape
    row = pl.BlockSpec((None, 1, D), lambda b, i: (b, 0, 0))
    full = lambda shape: pl.BlockSpec(shape, lambda b, i: (0,) * len(shape))
    return pl.pallas_call(
        _merge_kernel, name="merge",
        grid=(B, S // tm),
        in_specs=[pl.BlockSpec((None, tm, A_WIDTH), lambda b, i: (b, i, 0)),
                  pl.BlockSpec((None, tm, B_WIDTH), lambda b, i: (b, i, 0)),
                  pl.BlockSpec((None, tm, D), lambda b, i: (b, i, PB_GA // 2)),
                  pl.BlockSpec((None, tm, D), lambda b, i: (b, i, PB_GB // 2)),
                  pl.BlockSpec((None, tm, D), lambda b, i: (b, i, 0)),
                  row, full(wa.shape), full(wb.shape), full(wo.shape), full((1, D)), row, row],
        out_specs=[pl.BlockSpec((None, tm, D), lambda b, i: (b, i, 0)),
                   pl.BlockSpec((None, tm, D), lambda b, i: (b, i, 0))],
        out_shape=[jax.ShapeDtypeStruct((B, S, D), F32), jax.ShapeDtypeStruct((B, S, D), BF16)],
        compiler_params=_cparams(2),
    )(y_a, y_b, proj, proj, x, gt1, wa, wb, wo, n2g, sh2, sc2)


def _fold_kernel(sk_ref, wq_ref, o_ref):
    o_ref[...] = lax.dot_general(sk_ref[...], wq_ref[...], NT_DIMS, preferred_element_type=F32,
                                 precision=lax.Precision.HIGHEST).astype(o_ref.dtype)


def _fold(w_pq, sub_keys):
    D = w_pq.shape[0]
    dh = D_KEY // 2
    nhp = PEER_HEADS * 2
    return pl.pallas_call(
        _fold_kernel, name="fold",
        grid=(nhp,),
        in_specs=[pl.BlockSpec((None, N_KEYS, dh), lambda j: (j, 0, 0)),
                  pl.BlockSpec((D, dh), lambda j: (0, j))],
        out_specs=pl.BlockSpec((N_KEYS, D), lambda j: (j, 0)),
        out_shape=jax.ShapeDtypeStruct((nhp * N_KEYS, D), BF16),
        compiler_params=_cparams(1),
    )(sub_keys.reshape(nhp, N_KEYS, dh), w_pq)


_CELLS = [(r1, r2) for r1 in range(PEER_TOPK) for r2 in range(PEER_TOPK) if (r1 + 1) * (r2 + 1) <= PEER_TOPK]


def _batcher_pairs(lo, hi):
    def merge(lo, hi, r):
        step = r * 2
        if step < hi - lo:
            yield from merge(lo, hi, step)
            yield from merge(lo + r, hi, step)
            yield from [(i, i + r) for i in range(lo + r, hi - r, step)]
        else:
            yield (lo, lo + r)

    if hi - lo >= 1:
        mid = lo + (hi - lo) // 2
        yield from _batcher_pairs(lo, mid)
        yield from _batcher_pairs(mid + 1, hi)
        yield from merge(lo, hi, 1)


_KEY_GROUPS = N_KEYS // SUBLANES
_SORT_NET = list(_batcher_pairs(0, _KEY_GROUPS - 1))


def _select_kernel(tt, h2_ref, ws_ref, e1_ref, n1_ref, e2_ref, r2_ref,
                   s_scr, e1_scr, v_scr, n_scr, z_scr):
    nlt = tt // LANES
    s = lax.dot_general(ws_ref[...], h2_ref[...], NT_DIMS, preferred_element_type=F32)
    for lt in range(nlt):
        s_scr[lt] = s[:, lt * LANES:(lt + 1) * LANES]

    def extract(it, carry):
        h = it // nlt
        lt = it % nlt
        for p in range(2):
            sv = s_scr[lt, pl.ds(pl.multiple_of((h * 2 + p) * N_KEYS, N_KEYS), N_KEYS), :]
            cols = [sv[i * SUBLANES:(i + 1) * SUBLANES, :] for i in range(_KEY_GROUPS)]
            for i, j in _SORT_NET:
                cols[i], cols[j] = jnp.maximum(cols[i], cols[j]), jnp.minimum(cols[i], cols[j])
            tops = []
            for r in range(PEER_TOPK):
                m = jnp.max(cols[0], axis=0, keepdims=True)
                tops.append(m)
                v_scr[p, r, lt, pl.ds(h, 1), :] = m
                live = PEER_TOPK - 1 - r
                if live > 0:
                    eq = cols[0] == m
                    for i in range(live):
                        cols[i] = jnp.where(eq, cols[i + 1], cols[i])
            e = jnp.exp(sv - tops[0])
            if p == 0:
                e1_scr[h, lt] = e
            else:
                rank = jnp.full((N_KEYS, LANES), float(PEER_TOPK), F32)
                for r in reversed(range(PEER_TOPK)):
                    rank = jnp.where(sv >= tops[r], float(r), rank)
                r2_ref[h, lt] = rank.astype(r2_ref.dtype)
                e2_ref[h, lt] = e.astype(e2_ref.dtype)
        return carry

    lax.fori_loop(0, PEER_HEADS * nlt, extract, 0)

    def cells(lt, carry):
        v1 = [v_scr[0, r, lt] for r in range(PEER_TOPK)]
        v2 = [v_scr[1, r, lt] for r in range(PEER_TOPK)]
        vals = [v1[r1] + v2[r2] for (r1, r2) in _CELLS]
        nc = len(_CELLS)
        static = [0] * nc
        lost = [None] * nc
        won = [None] * nc
        for a in range(nc):
            for b in range(a + 1, nc):
                (a1, a2), (b1, b2) = _CELLS[a], _CELLS[b]
                if a1 <= b1 and a2 <= b2:
                    static[b] += 1
                else:
                    inc = jnp.where(vals[a] >= vals[b], 1.0, 0.0)
                    lost[b] = inc if lost[b] is None else lost[b] + inc
                    won[a] = inc if won[a] is None else won[a] + inc
                    static[a] += 1
        cnt = [None] * PEER_TOPK
        z = None
        for ci, (r1, r2) in enumerate(_CELLS):
            rk = float(static[ci])
            if lost[ci] is not None:
                rk = rk + lost[ci]
            if won[ci] is not None:
                rk = rk - won[ci]
            sel = jnp.where(rk < float(PEER_TOPK), 1.0, 0.0)
            cnt[r1] = sel if cnt[r1] is None else cnt[r1] + sel
            zt = sel * jnp.exp(vals[ci] - vals[0])
            z = zt if z is None else z + zt
        for r in range(PEER_TOPK):
            n_scr[r, lt] = cnt[r]
        z_scr[lt] = 1.0 / z
        return carry

    lax.fori_loop(0, nlt, cells, 0)

    for h in range(PEER_HEADS):
        for lt in range(nlt):
            s1 = s_scr[lt, 2 * h * N_KEYS:(2 * h + 1) * N_KEYS, :]
            n1 = jnp.zeros((N_KEYS, LANES), F32)
            for r in reversed(range(PEER_TOPK)):
                n1 = jnp.where(s1 >= v_scr[0, r, lt, h:h + 1, :], n_scr[r, lt, h:h + 1, :], n1)
            e1 = e1_scr[h, lt] * z_scr[lt, h:h + 1, :]
            n1_ref[lt, pl.ds(h, N_KEYS, stride=PEER_HEADS), :] = n1
            e1_ref[lt, pl.ds(h, N_KEYS, stride=PEER_HEADS), :] = e1


def _select(h2, ws, tt):
    T, D = h2.shape
    nlt = tt // LANES
    rows = N_KEYS * PEER_HEADS
    return pl.pallas_call(
        functools.partial(_select_kernel, tt), name="select",
        grid=(T // tt,),
        in_specs=[pl.BlockSpec((tt, D), lambda i: (i, 0)),
                  pl.BlockSpec(ws.shape, lambda i: (0, 0))],
        out_specs=[pl.BlockSpec((nlt, rows, LANES), lambda i: (i, 0, 0)),
                   pl.BlockSpec((nlt, rows, LANES), lambda i: (i, 0, 0)),
                   pl.BlockSpec((PEER_HEADS, nlt, N_KEYS, LANES), lambda i: (0, i, 0, 0)),
                   pl.BlockSpec((PEER_HEADS, nlt, N_KEYS, LANES), lambda i: (0, i, 0, 0))],
        out_shape=[jax.ShapeDtypeStruct((T // LANES, rows, LANES), F32),
                   jax.ShapeDtypeStruct((T // LANES, rows, LANES), F32),
                   jax.ShapeDtypeStruct((PEER_HEADS, T // LANES, N_KEYS, LANES), BF16),
                   jax.ShapeDtypeStruct((PEER_HEADS, T // LANES, N_KEYS, LANES), BF16)],
        scratch_shapes=[pltpu.VMEM((nlt, 2 * PEER_HEADS * N_KEYS, LANES), F32),
                        pltpu.VMEM((PEER_HEADS, nlt, N_KEYS, LANES), F32),
                        pltpu.VMEM((2, PEER_TOPK, nlt, PEER_HEADS, LANES), F32),
                        pltpu.VMEM((PEER_TOPK, nlt, PEER_HEADS, LANES), F32),
                        pltpu.VMEM((nlt, PEER_HEADS, LANES), F32)],
        compiler_params=_cparams(1),
    )(h2, ws)


def _gelu_tanh_sigmoid_form(x):
    k0 = -2.0 * 0.7978845608028654 * 1.4426950408889634
    k1 = k0 * 0.044715
    return x * (1.0 / (1.0 + jnp.exp2(x * (k1 * (x * x) + k0))))


DENSE_CW = 2 * LANES
DENSE_RS = 2 * LANES


def _dense_kernel(eb, final_norm, h2_ref, u_ref, vT_ref, e1_ref, n1_ref, e2_ref, r2_ref,
                  x1_ref, gt_ref, fg_ref, o_ref, a_scr, g_scr, acc_scr):
    j = pl.program_id(2)
    nchunk, d_model, cw = acc_scr.shape
    lpc = cw // LANES
    rs = DENSE_RS
    nsl = eb // rs
    assert d_model // rs == nsl

    @pl.when(j == 0)
    def _():
        acc_scr[...] = jnp.zeros_like(acc_scr)

    def slice_tiles(i):
        return [(k, l) for k in range(i * rs // N_KEYS, (i + 1) * rs // N_KEYS) for l in range(lpc)]

    def score_slice(c, i):
        a = lax.dot_general(u_ref[i * rs:(i + 1) * rs, :], h2_ref[c * cw:(c + 1) * cw, :], NT_DIMS,
                            preferred_element_type=F32)
        a_scr[c % 2, i * rs:(i + 1) * rs, :] = _gelu_tanh_sigmoid_form(a.astype(BF16))

    def weight_slice(c, i):
        for k, l in slice_tiles(i):
            rows = slice(k * N_KEYS, (k + 1) * N_KEYS)
            heads = slice(k * PEER_HEADS, (k + 1) * PEER_HEADS)
            lanes = slice(l * LANES, (l + 1) * LANES)
            lt = c * lpc + l
            e1_all = e1_ref[lt, heads, :]
            n1_all = n1_ref[lt, heads, :]
            gate = None
            for h in range(PEER_HEADS):
                e1 = jnp.broadcast_to(e1_all[h:h + 1, :], (N_KEYS, LANES)).astype(BF16)
                n1 = jnp.broadcast_to(n1_all[h:h + 1, :], (N_KEYS, LANES)).astype(BF16)
                term = jnp.minimum(e1 * e2_ref[h, lt], jnp.maximum(n1 - r2_ref[h, lt], jnp.zeros((), BF16)))
                gate = term if gate is None else gate + term
            g_scr[c % 2, rows, lanes] = gate * a_scr[c % 2, rows, lanes]

    def output_slice(c, i):
        acc_scr[c, i * rs:(i + 1) * rs, :] += jnp.dot(vT_ref[i * rs:(i + 1) * rs, :], g_scr[c % 2],
                                                      preferred_element_type=F32)

    for c in range(nchunk + 1):
        for i in range(nsl):
            if c < nchunk:
                score_slice(c, i)
            if c > 0:
                output_slice(c - 1, i)
            if c < nchunk and i > 0:
                weight_slice(c, i - 1)
        if c < nchunk:
            weight_slice(c, nsl - 1)

    @pl.when(j == pl.num_programs(2) - 1)
    def _():
        for c in range(nchunk):
            tok = slice(c * cw, (c + 1) * cw)
            x2 = x1_ref[tok, :] + gt_ref[...] * acc_scr[c].T
            if final_norm:
                x2 = x2 * lax.rsqrt(jnp.mean(x2 * x2, axis=-1, keepdims=True) + EPS) * fg_ref[...]
            o_ref[tok, :] = x2


def _dense(h2, u_bf, vT_bf, e1n, n1, e2, r2, x1, gt2, fg, final_norm, tt, eb):
    B, S, D = x1.shape
    E = u_bf.shape[0]
    nt = S // tt
    rb = eb // N_KEYS * PEER_HEADS
    cw = min(tt, DENSE_CW)
    assert eb == D and eb % DENSE_RS == 0
    gate_spec = pl.BlockSpec((tt // LANES, rb, LANES), lambda b, i, j: (b * nt + i, j, 0))
    tile_spec = pl.BlockSpec((PEER_HEADS, tt // LANES, N_KEYS, LANES), lambda b, i, j: (0, b * nt + i, 0, 0))
    return pl.pallas_call(
        functools.partial(_dense_kernel, eb, final_norm), name="dense",
        grid=(B, nt, E // eb),
        in_specs=[pl.BlockSpec((None, tt, D), lambda b, i, j: (b, i, 0)),
                  pl.BlockSpec((eb, D), lambda b, i, j: (j, 0)),
                  pl.BlockSpec((D, eb), lambda b, i, j: (0, j)),
                  gate_spec, gate_spec, tile_spec, tile_spec,
                  pl.BlockSpec((None, tt, D), lambda b, i, j: (b, i, 0)),
                  pl.BlockSpec((None, 1, D), lambda b, i, j: (b, 0, 0)),
                  pl.BlockSpec((1, D), lambda b, i, j: (0, 0))],
        out_specs=pl.BlockSpec((None, tt, D), lambda b, i, j: (b, i, 0)),
        out_shape=jax.ShapeDtypeStruct((B, S, D), F32),
        scratch_shapes=[pltpu.VMEM((2, eb, cw), BF16), pltpu.VMEM((2, eb, cw), BF16),
                        pltpu.VMEM((tt // cw, D, cw), F32)],
        compiler_params=_cparams(3),
    )(h2, u_bf, vT_bf, e1n, n1, e2, r2, x1, gt2, fg)


def _tile(n, pref):
    return pref if n % pref == 0 else n


def kernel(x, c, w_ada, b_ada, norm1_g, w_in, conv_w, conv_b, b_igate, b_fgate, rel_bias, mlstm_norm_g,
           w_branch_a, w_branch_b, w_out, norm2_g, w_peer_q, peer_sub_keys, peer_u, peer_v, final_g):
    B, S, D = x.shape
    depth = w_ada.shape[0]
    for l in range(depth):
        mod = _mod(c, w_ada[l], b_ada[l])
        sh1, sc1, gt1, sh2, sc2, gt2 = [m.reshape(B, 1, D) for m in jnp.split(mod, 6, axis=-1)]

        o = [0]
        for sz in (A_WIDTH, A_WIDTH, A_WIDTH, 2 * B_WIDTH, B_WIDTH, B_WIDTH, B_HEADS, B_HEADS, D, D):
            o.append(o[-1] + sz)
        wl = w_in[l]
        seg = lambda i: wl[:, o[i]:o[i + 1]]
        w_main = jnp.concatenate([seg(8), seg(9), seg(0), seg(1), seg(2), seg(3), seg(4), seg(5)], axis=1).astype(BF16)
        w_gate = jnp.concatenate([seg(6), seg(7), jnp.zeros((D, LANES - 2 * B_HEADS), F32)], axis=1).astype(BF16)
        w_gateT = w_gate[:, :SUBLANES].T
        gb = jnp.concatenate([b_igate[l], b_fgate[l]]).astype(F32)
        tm = _tile(S, 512)
        gbias = jnp.concatenate([gb, jnp.zeros((LANES - 2 * B_HEADS,), F32)]).reshape(1, LANES)
        gbiasT = jnp.broadcast_to(gb[:, None], (SUBLANES, tm))
        proj, gates, gatesT = _inproj(x, sh1, sc1, norm1_g[l].reshape(1, D), w_main, w_gate, w_gateT,
                                      gbias, gbiasT, tm, P_COLS)

        tq = _tile(S, 256)
        y_a = _attention(proj, rel_bias[l], tq, 2 if (S // tq) % 2 == 0 else 1)
        y_b = _mlstm(proj, gates, gatesT, conv_w[l], conv_b[l], mlstm_norm_g[l], _tile(S, 128), _tile(B, 2))
        x1, h2 = _merge(y_a, y_b, proj, x, gt1, w_branch_a[l].astype(BF16), w_branch_b[l].astype(BF16),
                        w_out[l].astype(BF16), norm2_g[l].reshape(1, D), sh2, sc2, _tile(S, 512))

        ws = _fold(w_peer_q[l], peer_sub_keys[l])
        e1n, n1, e2, r2 = _select(h2.reshape(B * S, D), ws, _tile(B * S, 512))
        x = _dense(h2, peer_u[l].astype(BF16), peer_v[l].T.astype(BF16), e1n, n1, e2, r2, x1, gt2,
                   final_g.reshape(1, D), l == depth - 1, _tile(S, 1024), 1024)
    return x
```

```python
import functools

import jax
import jax.numpy as jnp
from jax import lax
from jax.experimental import pallas as pl
from jax.experimental.pallas import tpu as pltpu

F32 = jnp.float32
BF16 = jnp.bfloat16

CHUNK = 64
LEFT_CHUNKS = 8
A_HEADS = 8
A_HEAD_DIM = 64
A_WIDTH = A_HEADS * A_HEAD_DIM
REL_CLIP = 128
B_HEADS = 4
B_HEAD_DIM = 128
B_WIDTH = B_HEADS * B_HEAD_DIM
CONV_K = 4
PEER_HEADS = 8
N_KEYS = 128
PEER_TOPK = 16
D_KEY = 256
EPS = 1e-6

LANES = 128
SUBLANES = 8
NEG = -1e30
VMEM_LIMIT = 56 * 1024 * 1024

PB_GA, PB_GB, PB_QA, PB_KA, PB_VA, PB_QP, PB_KP, PB_VB, PB_OB = 0, 2, 4, 5, 6, 7, 8, 9, 10
P_COLS = 11 * 512

NT_DIMS = (((1,), (1,)), ((), ()))
TN_DIMS = (((0,), (0,)), ((), ()))


def _cparams(n_axes, flags=None):
    return pltpu.CompilerParams(dimension_semantics=("arbitrary",) * n_axes, vmem_limit_bytes=VMEM_LIMIT,
                                flags=flags)


def _sigmoid(x):
    return 1.0 / (1.0 + jnp.exp(-x))


def _log_sigmoid(x):
    return jnp.minimum(x, 0.0) - jnp.log(1.0 + jnp.exp(-jnp.abs(x)))


def _mod_kernel(c_ref, w_ref, b_ref, o_ref):
    c = c_ref[...]
    o_ref[...] = jnp.dot(c * _sigmoid(c), w_ref[...], preferred_element_type=F32,
                         precision=lax.Precision.HIGHEST) + b_ref[...]


def _mod(c, w, b):
    B, D = c.shape
    N = w.shape[1]
    tn = 512
    return pl.pallas_call(
        _mod_kernel, name="mod",
        grid=(N // tn,),
        in_specs=[pl.BlockSpec((B, D), lambda j: (0, 0)),
                  pl.BlockSpec((D, tn), lambda j: (0, j)),
                  pl.BlockSpec((1, tn), lambda j: (0, j))],
        out_specs=pl.BlockSpec((B, tn), lambda j: (0, j)),
        out_shape=jax.ShapeDtypeStruct((B, N), F32),
        compiler_params=_cparams(1),
    )(c, w, b.reshape(1, N))


def _inproj_kernel(x_ref, sh_ref, sc_ref, g_ref, w_ref, wg_ref, wgT_ref, gb_ref, gbT_ref,
                   o_ref, og_ref, ogT_ref, h_scr):
    @pl.when(pl.program_id(2) == 0)
    def _():
        x = x_ref[...]
        y = x * lax.rsqrt(jnp.mean(x * x, axis=-1, keepdims=True) + EPS) * g_ref[...]
        hb = (y * (1.0 + sc_ref[...]) + sh_ref[...]).astype(BF16)
        h_scr[...] = hb
        og_ref[...] = jnp.dot(hb, wg_ref[...], preferred_element_type=F32) + gb_ref[...]
        ogT_ref[...] = lax.dot_general(wgT_ref[...], hb, NT_DIMS, preferred_element_type=F32) + gbT_ref[...]

    o_ref[...] = jnp.dot(h_scr[...], w_ref[...], preferred_element_type=F32).astype(o_ref.dtype)


def _inproj(x, sh, sc, g, w_main, w_gate, w_gateT, gbias, gbiasT, tm, tn):
    B, S, D = x.shape
    N = w_main.shape[1]
    row = lambda b, i, j: (b, 0, 0)
    return pl.pallas_call(
        _inproj_kernel, name="inproj",
        grid=(B, S // tm, N // tn),
        in_specs=[pl.BlockSpec((None, tm, D), lambda b, i, j: (b, i, 0)),
                  pl.BlockSpec((None, 1, D), row),
                  pl.BlockSpec((None, 1, D), row),
                  pl.BlockSpec((1, D), lambda b, i, j: (0, 0)),
                  pl.BlockSpec((D, tn), lambda b, i, j: (0, j)),
                  pl.BlockSpec((D, LANES), lambda b, i, j: (0, 0)),
                  pl.BlockSpec((SUBLANES, D), lambda b, i, j: (0, 0)),
                  pl.BlockSpec((1, LANES), lambda b, i, j: (0, 0)),
                  pl.BlockSpec((SUBLANES, tm), lambda b, i, j: (0, 0))],
        out_specs=[pl.BlockSpec((None, tm, tn), lambda b, i, j: (b, i, j)),
                   pl.BlockSpec((None, tm, LANES), lambda b, i, j: (b, i, 0)),
                   pl.BlockSpec((None, SUBLANES, tm), lambda b, i, j: (b, 0, i))],
        out_shape=[jax.ShapeDtypeStruct((B, S, N), BF16),
                   jax.ShapeDtypeStruct((B, S, LANES), F32),
                   jax.ShapeDtypeStruct((B, SUBLANES, S), F32)],
        scratch_shapes=[pltpu.VMEM((tm, D), BF16)],
        compiler_params=_cparams(3),
    )(x, sh, sc, g, w_main, w_gate, w_gateT, gbias, gbiasT)


def _attn_kernel(nprev, nq, q_ref, *refs):
    nkv = nprev + nq
    k_refs = refs[:nkv]
    v_refs = refs[nkv:2 * nkv]
    bias_ref, o_ref = refs[2 * nkv:]
    tq = q_ref.shape[0] // nq
    i = pl.program_id(2)
    lo = lax.broadcasted_iota(jnp.int32, (1, LANES), 1) < A_HEAD_DIM
    scale = A_HEAD_DIM ** -0.5
    probs = [(qq, hh) for qq in range(nq) for hh in range(2)]
    qm = []
    for qq, hh in probs:
        q = q_ref[qq * tq:(qq + 1) * tq, :]
        qm.append(jnp.where(lo if hh == 0 else jnp.logical_not(lo), q, jnp.zeros_like(q)))
    ss = [[] for _ in probs]
    for c in range(nprev + 1):
        for n, (qq, hh) in enumerate(probs):
            s = lax.dot_general(qm[n], k_refs[qq + c][...], NT_DIMS, preferred_element_type=F32) * scale
            s = s + bias_ref[hh, c]
            if c < nprev:
                s = jnp.where(i * nq + qq >= nprev - c, s, NEG)
            ss[n].append(s)
    m = [functools.reduce(jnp.maximum, [jnp.max(s, axis=-1, keepdims=True) for s in ss[n]]) for n in range(len(probs))]
    l = [None for _ in probs]
    acc = [None for _ in probs]
    for c in range(nprev + 1):
        for n, (qq, hh) in enumerate(probs):
            p = jnp.exp(ss[n][c] - m[n])
            pl_ = jnp.sum(p, axis=-1, keepdims=True)
            pv = jnp.dot(p.astype(BF16), v_refs[qq + c][...], preferred_element_type=F32)
            l[n] = pl_ if l[n] is None else l[n] + pl_
            acc[n] = pv if acc[n] is None else acc[n] + pv
    for qq in range(nq):
        a0, a1 = 2 * qq, 2 * qq + 1
        o_ref[qq * tq:(qq + 1) * tq, :] = jnp.where(lo, acc[a0] / l[a0], acc[a1] / l[a1]).astype(o_ref.dtype)


def _attn_bias(rel_bias, tq):
    nprev = LEFT_CHUNKS * CHUNK // tq
    nk = (nprev + 1) * tq
    ql = jnp.arange(tq)[:, None]
    kl = jnp.arange(nk)[None, :] - nprev * tq
    L = nk + tq
    delta = jnp.arange(L)
    delta = jnp.where(delta < nk, delta, delta - L)
    w = rel_bias[:, jnp.clip(nprev * tq - delta, -REL_CLIP, REL_CLIP) + REL_CLIP].astype(F32)
    toep = jnp.tile(w, (1, tq))[:, :tq * (L - 1)].reshape(A_HEADS, tq, L - 1)[:, :, :nk]
    qc = ql // CHUNK
    kc = jnp.floor_divide(kl, CHUNK)
    vis = (kc <= qc) & (kc >= qc - LEFT_CHUNKS)
    bias = jnp.where(vis[None], toep, NEG)
    return bias.reshape(A_HEADS, tq, nprev + 1, tq).transpose(0, 2, 1, 3)


def _attention(proj, rel_bias, tq, nq):
    B, S, _ = proj.shape
    nprev = LEFT_CHUNKS * CHUNK // tq
    nkv = nprev + nq
    bias = _attn_bias(rel_bias, tq)
    npair = A_HEADS // 2
    qb, kb, vb = PB_QA * 4, PB_KA * 4, PB_VA * 4

    def kv_spec(base, c):
        back = nprev - c
        return pl.BlockSpec((None, tq, LANES), lambda p, b, i: (b, jnp.maximum(i * nq - back, 0), base + p))

    in_specs = [pl.BlockSpec((None, nq * tq, LANES), lambda p, b, i: (b, i, qb + p))]
    in_specs += [kv_spec(kb, c) for c in range(nkv)]
    in_specs += [kv_spec(vb, c) for c in range(nkv)]
    in_specs += [pl.BlockSpec((2, nprev + 1, tq, tq), lambda p, b, i: (p, 0, 0, 0))]
    return pl.pallas_call(
        functools.partial(_attn_kernel, nprev, nq), name="attn",
        grid=(npair, B, S // (nq * tq)),
        in_specs=in_specs,
        out_specs=pl.BlockSpec((None, nq * tq, LANES), lambda p, b, i: (b, i, p)),
        out_shape=jax.ShapeDtypeStruct((B, S, A_WIDTH), BF16),
        compiler_params=_cparams(3),
    )(*([proj] * (2 * nkv + 1)), bias)


MLSTM_HEAD_GROUP = 4


def _mlstm_kernel(nb, rows, qp_ref, kp_ref, v_ref, ob_ref, g_ref, gT_ref, cw_ref, cb_ref, ng_ref, tril_ref, triu_ref,
                  o_ref, C_scr, n_scr, m_scr, ext_scr):
    @pl.when(pl.program_id(1) == 0)
    def _():
        C_scr[...] = jnp.zeros_like(C_scr)
        n_scr[...] = jnp.zeros_like(n_scr)
        m_scr[...] = jnp.zeros_like(m_scr)
        ext_scr[:, :, 0:SUBLANES, :] = jnp.zeros((nb, 2, SUBLANES, B_WIDTH), F32)

    q_all, k_all = [], []
    for bb in range(nb):
        qk = []
        for t, ref in enumerate((qp_ref, kp_ref)):
            ext_scr[bb, t, SUBLANES:SUBLANES + rows, :] = ref[bb].astype(F32)
            acc = cb_ref[:, t * B_WIDTH:(t + 1) * B_WIDTH]
            for kk in range(CONV_K):
                start = SUBLANES - (CONV_K - 1) + kk
                acc = acc + cw_ref[kk:kk + 1, t * B_WIDTH:(t + 1) * B_WIDTH] * ext_scr[bb, t, start:start + rows, :]
            ext_scr[bb, t, 0:SUBLANES, :] = ext_scr[bb, t, rows:rows + SUBLANES, :]
            qk.append(acc * _sigmoid(acc))
        q_all.append(qk[0])
        k_all.append(qk[1] * (B_HEAD_DIM ** -0.5))

    causal = (lax.broadcasted_iota(jnp.int32, (CHUNK, CHUNK), 0) >= lax.broadcasted_iota(jnp.int32, (CHUNK, CHUNK), 1))
    hi = lax.Precision.HIGHEST
    each = lambda f, *cols: [f(*a) for a in zip(*cols)]

    for c in range(rows // CHUNK):
        r0 = c * CHUNK
        tsl = slice(r0, r0 + CHUNK)
        for hg in range(0, B_HEADS, MLSTM_HEAD_GROUP):
            chains = [(bb, h) for h in range(hg, hg + MLSTM_HEAD_GROUP) for bb in range(nb)]
            hsl = [slice(h * B_HEAD_DIM, (h + 1) * B_HEAD_DIM) for _, h in chains]
            slot = [bb * B_HEADS + h for bb, h in chains]
            qh = [q_all[bb][tsl, sl] for (bb, _), sl in zip(chains, hsl)]
            kh = [k_all[bb][tsl, sl] for (bb, _), sl in zip(chains, hsl)]
            vh = [v_ref[bb, tsl, sl] for (bb, _), sl in zip(chains, hsl)]
            qh_b = each(lambda x: x.astype(BF16), qh)
            li_b = [jnp.broadcast_to(g_ref[bb, tsl, h:h + 1], (CHUNK, LANES)) for bb, h in chains]
            lf_b = [_log_sigmoid(jnp.broadcast_to(g_ref[bb, tsl, B_HEADS + h:B_HEADS + h + 1], (CHUNK, LANES)))
                    for bb, h in chains]
            li_r = [jnp.broadcast_to(gT_ref[bb, h:h + 1, tsl], (CHUNK, CHUNK)) for bb, h in chains]
            lf_r = [_log_sigmoid(jnp.broadcast_to(gT_ref[bb, B_HEADS + h:B_HEADS + h + 1, tsl], (CHUNK, CHUNK)))
                    for bb, h in chains]
            a_b = each(lambda x: jnp.dot(tril_ref[...], x, preferred_element_type=F32, precision=hi), lf_b)
            a_r = each(lambda x: jnp.dot(x, triu_ref[...], preferred_element_type=F32, precision=hi), lf_r)
            D = each(lambda ab, ar, lr: jnp.where(causal, ab[:, :CHUNK] - ar + lr, NEG), a_b, a_r, li_r)

            m_prev = [m_scr[s, 0:1, :] for s in slot]
            inter = each(lambda ab, mp: ab + mp, a_b, m_prev)
            m_row = each(lambda it, d: jnp.maximum(it, jnp.max(d, axis=-1, keepdims=True)), inter, D)
            w_inter = each(lambda it, mr: jnp.exp(it - mr), inter, m_row)
            Wd = each(lambda d, mr: jnp.exp(d - mr[:, :CHUNK]), D, m_row)
            qkw = each(lambda q, k, w: lax.dot_general(q, k.astype(BF16), NT_DIMS, preferred_element_type=F32) * w,
                       qh_b, kh, Wd)
            qC = [jnp.dot(q, C_scr[s].astype(BF16), preferred_element_type=F32) for q, s in zip(qh_b, slot)]
            qkv = each(lambda p, v: jnp.dot(p.astype(BF16), v, preferred_element_type=F32), qkw, vh)
            num = each(lambda w, a, b: w * a + b, w_inter, qC, qkv)
            qn = [jnp.sum(q * n_scr[s, 0:1, :], axis=-1, keepdims=True) for q, s in zip(qh, slot)]
            den = each(lambda w, a, p: w[:, 0:1] * a + jnp.sum(p, axis=-1, keepdims=True), w_inter, qn, qkw)
            hh = each(lambda nu, de, mr: nu / jnp.maximum(jnp.abs(de), jnp.exp(-mr[:, 0:1])), num, den, m_row)

            A_row = each(lambda ab: ab[CHUNK - 1:CHUNK, :], a_b)
            g_b = each(lambda A, ab, lb: A - ab + lb, A_row, a_b, li_b)
            m_new = each(lambda A, mp, g: jnp.maximum(A + mp, jnp.max(g, axis=0, keepdims=True)), A_row, m_prev, g_b)
            decay = each(lambda A, mp, mn: jnp.exp(A + mp - mn), A_row, m_prev, m_new)
            kw = each(lambda g, mn, k: jnp.exp(g - mn) * k, g_b, m_new, kh)
            dC = each(lambda w, v: lax.dot_general(w.astype(BF16), v, TN_DIMS, preferred_element_type=F32), kw, vh)
            for s, de, d, w, mn in zip(slot, decay, dC, kw, m_new):
                C_scr[s] = de * C_scr[s] + d
                n_scr[s] = jnp.broadcast_to(de * n_scr[s, 0:1, :] + jnp.sum(w, axis=0, keepdims=True),
                                            (SUBLANES, LANES))
                m_scr[s] = jnp.broadcast_to(mn, (SUBLANES, LANES))

            for (bb, _), sl, x in zip(chains, hsl, hh):
                hn = x * lax.rsqrt(jnp.mean(x * x, axis=-1, keepdims=True) + EPS) * ng_ref[:, sl]
                ob = ob_ref[bb, tsl, sl].astype(F32)
                o_ref[bb, tsl, sl] = (_sigmoid(ob) * hn).astype(o_ref.dtype)


def _mlstm(proj, gates, gatesT, conv_w, conv_b, norm_g, rows, nb):
    B, S, _ = proj.shape
    tril = jnp.tril(jnp.ones((CHUNK, CHUNK), F32))
    blk = lambda cb: pl.BlockSpec((nb, rows, B_WIDTH), lambda b, j: (b, j, cb))
    full = lambda shape: pl.BlockSpec(shape, lambda b, j: (0,) * len(shape))
    return pl.pallas_call(
        functools.partial(_mlstm_kernel, nb, rows), name="mlstm",
        grid=(B // nb, S // rows),
        in_specs=[blk(PB_QP), blk(PB_KP), blk(PB_VB), blk(PB_OB),
                  pl.BlockSpec((nb, rows, LANES), lambda b, j: (b, j, 0)),
                  pl.BlockSpec((nb, SUBLANES, rows), lambda b, j: (b, 0, j)),
                  full((CONV_K, 2 * B_WIDTH)), full((1, 2 * B_WIDTH)), full((1, B_WIDTH)),
                  full((CHUNK, CHUNK)), full((CHUNK, CHUNK))],
        out_specs=pl.BlockSpec((nb, rows, B_WIDTH), lambda b, j: (b, j, 0)),
        out_shape=jax.ShapeDtypeStruct((B, S, B_WIDTH), BF16),
        scratch_shapes=[pltpu.VMEM((nb * B_HEADS, B_HEAD_DIM, B_HEAD_DIM), F32),
                        pltpu.VMEM((nb * B_HEADS, SUBLANES, LANES), F32),
                        pltpu.VMEM((nb * B_HEADS, SUBLANES, LANES), F32),
                        pltpu.VMEM((nb, 2, rows + SUBLANES, B_WIDTH), F32)],
        compiler_params=_cparams(2),
    )(proj, proj, proj, proj, gates, gatesT, conv_w, conv_b.reshape(1, -1), norm_g.reshape(1, -1), tril, tril.T)


def _merge_kernel(ya_ref, yb_ref, ga_ref, gb_ref, x_ref, gt_ref, wa_ref, wb_ref, wo_ref, n2_ref, sh_ref, sc_ref,
                  x1_ref, h2_ref):
    ya = jnp.dot(ya_ref[...], wa_ref[...], preferred_element_type=F32)
    yb = jnp.dot(yb_ref[...], wb_ref[...], preferred_element_type=F32)
    y = _sigmoid(ga_ref[...].astype(F32)) * ya + _sigmoid(gb_ref[...].astype(F32)) * yb
    x1 = x_ref[...] + gt_ref[...] * jnp.dot(y.astype(BF16), wo_ref[...], preferred_element_type=F32)
    x1_ref[...] = x1
    yn = x1 * lax.rsqrt(jnp.mean(x1 * x1, axis=-1, keepdims=True) + EPS) * n2_ref[...]
    h2_ref[...] = (yn * (1.0 + sc_ref[...]) + sh_ref[...]).astype(h2_ref.dtype)


def _merge(y_a, y_b, proj, x, gt1, wa, wb, wo, n2g, sh2, sc2, tm):
    B, S, D = x.shape
    row = pl.BlockSpec((None, 1, D), lambda b, i: (b, 0, 0))
    full = lambda shape: pl.BlockSpec(shape, lambda b, i: (0,) * len(shape))
    return pl.pallas_call(
        _merge_kernel, name="merge",
        grid=(B, S // tm),
        in_specs=[pl.BlockSpec((None, tm, A_WIDTH), lambda b, i: (b, i, 0)),
                  pl.BlockSpec((None, tm, B_WIDTH), lambda b, i: (b, i, 0)),
                  pl.BlockSpec((None, tm, D), lambda b, i: (b, i, PB_GA // 2)),
                  pl.BlockSpec((None, tm, D), lambda b, i: (b, i, PB_GB // 2)),
                  pl.BlockSpec((None, tm, D), lambda b, i: (b, i, 0)),
                  row, full(wa.shape), full(wb.shape), full(wo.shape), full((1, D)), row, row],
        out_specs=[pl.BlockSpec((None, tm, D), lambda b, i: (b, i, 0)),
                   pl.BlockSpec((None, tm, D), lambda b, i: (b, i, 0))],
        out_shape=[jax.ShapeDtypeStruct((B, S, D), F32), jax.ShapeDtypeStruct((B, S, D), BF16)],
        compiler_params=_cparams(2),
    )(y_a, y_b, proj, proj, x, gt1, wa, wb, wo, n2g, sh2, sc2)


def _fold_kernel(sk_ref, wq_ref, o_ref):
    o_ref[...] = lax.dot_general(sk_ref[...], wq_ref[...], NT_DIMS, preferred_element_type=F32,
                                 precision=lax.Precision.HIGHEST).astype(o_ref.dtype)


def _fold(w_pq, sub_keys):
    D = w_pq.shape[0]
    dh = D_KEY // 2
    nhp = PEER_HEADS * 2
    return pl.pallas_call(
        _fold_kernel, name="fold",
        grid=(nhp,),
        in_specs=[pl.BlockSpec((None, N_KEYS, dh), lambda j: (j, 0, 0)),
                  pl.BlockSpec((D, dh), lambda j: (0, j))],
        out_specs=pl.BlockSpec((N_KEYS, D), lambda j: (j, 0)),
        out_shape=jax.ShapeDtypeStruct((nhp * N_KEYS, D), BF16),
        compiler_params=_cparams(1),
    )(sub_keys.reshape(nhp, N_KEYS, dh), w_pq)


_CELLS = [(r1, r2) for r1 in range(PEER_TOPK) for r2 in range(PEER_TOPK) if (r1 + 1) * (r2 + 1) <= PEER_TOPK]


def _batcher_pairs(lo, hi):
    def merge(lo, hi, r):
        step = r * 2
        if step < hi - lo:
            yield from merge(lo, hi, step)
            yield from merge(lo + r, hi, step)
            yield from [(i, i + r) for i in range(lo + r, hi - r, step)]
        else:
            yield (lo, lo + r)

    if hi - lo >= 1:
        mid = lo + (hi - lo) // 2
        yield from _batcher_pairs(lo, mid)
        yield from _batcher_pairs(mid + 1, hi)
        yield from merge(lo, hi, 1)


_KEY_GROUPS = N_KEYS // SUBLANES
_SORT_NET = list(_batcher_pairs(0, _KEY_GROUPS - 1))


def _select_kernel(tt, h2_ref, ws_ref, e1_ref, n1_ref, e2_ref, r2_ref,
                   s_scr, e1_scr, v_scr, n_scr, z_scr):
    nlt = tt // LANES
    s = lax.dot_general(ws_ref[...], h2_ref[...], NT_DIMS, preferred_element_type=F32)
    for lt in range(nlt):
        s_scr[lt] = s[:, lt * LANES:(lt + 1) * LANES]

    def extract(it, carry):
        h = it // nlt
        lt = it % nlt
        for p in range(2):
            sv = s_scr[lt, pl.ds(pl.multiple_of((h * 2 + p) * N_KEYS, N_KEYS), N_KEYS), :]
            cols = [sv[i * SUBLANES:(i + 1) * SUBLANES, :] for i in range(_KEY_GROUPS)]
            for i, j in _SORT_NET:
                cols[i], cols[j] = jnp.maximum(cols[i], cols[j]), jnp.minimum(cols[i], cols[j])
            tops = []
            for r in range(PEER_TOPK):
                m = jnp.max(cols[0], axis=0, keepdims=True)
                tops.append(m)
                v_scr[p, r, lt, pl.ds(h, 1), :] = m
                live = PEER_TOPK - 1 - r
                if live > 0:
                    eq = cols[0] == m
                    for i in range(live):
                        cols[i] = jnp.where(eq, cols[i + 1], cols[i])
            e = jnp.exp(sv - tops[0])
            if p == 0:
                e1_scr[h, lt] = e
            else:
                rank = jnp.full((N_KEYS, LANES), float(PEER_TOPK), F32)
                for r in reversed(range(PEER_TOPK)):
                    rank = jnp.where(sv >= tops[r], float(r), rank)
                r2_ref[h, lt] = rank.astype(r2_ref.dtype)
                e2_ref[h, lt] = e.astype(e2_ref.dtype)
        return carry

    lax.fori_loop(0, PEER_HEADS * nlt, extract, 0)

    def cells(lt, carry):
        v1 = [v_scr[0, r, lt] for r in range(PEER_TOPK)]
        v2 = [v_scr[1, r, lt] for r in range(PEER_TOPK)]
        vals = [v1[r1] + v2[r2] for (r1, r2) in _CELLS]
        nc = len(_CELLS)
        static = [0] * nc
        lost = [None] * nc
        won = [None] * nc
        for a in range(nc):
            for b in range(a + 1, nc):
                (a1, a2), (b1, b2) = _CELLS[a], _CELLS[b]
                if a1 <= b1 and a2 <= b2:
                    static[b] += 1
                else:
                    inc = jnp.where(vals[a] >= vals[b], 1.0, 0.0)
                    lost[b] = inc if lost[b] is None else lost[b] + inc
                    won[a] = inc if won[a] is None else won[a] + inc
                    static[a] += 1
        cnt = [None] * PEER_TOPK
        z = None
        for ci, (r1, r2) in enumerate(_CELLS):
            rk = float(static[ci])
            if lost[ci] is not None:
                rk = rk + lost[ci]
            if won[ci] is not None:
                rk = rk - won[ci]
            sel = jnp.where(rk < float(PEER_TOPK), 1.0, 0.0)
            cnt[r1] = sel if cnt[r1] is None else cnt[r1] + sel
            zt = sel * jnp.exp(vals[ci] - vals[0])
            z = zt if z is None else z + zt
        for r in range(PEER_TOPK):
            n_scr[r, lt] = cnt[r]
        z_scr[lt] = 1.0 / z
        return carry

    lax.fori_loop(0, nlt, cells, 0)

    for h in range(PEER_HEADS):
        for lt in range(nlt):
            s1 = s_scr[lt, 2 * h * N_KEYS:(2 * h + 1) * N_KEYS, :]
            n1 = jnp.zeros((N_KEYS, LANES), F32)
            for r in reversed(range(PEER_TOPK)):
                n1 = jnp.where(s1 >= v_scr[0, r, lt, h:h + 1, :], n_scr[r, lt, h:h + 1, :], n1)
            e1 = e1_scr[h, lt] * z_scr[lt, h:h + 1, :]
            n1_ref[lt, pl.ds(h, N_KEYS, stride=PEER_HEADS), :] = n1
            e1_ref[lt, pl.ds(h, N_KEYS, stride=PEER_HEADS), :] = e1


def _select(h2, ws, tt):
    T, D = h2.shape
    nlt = tt // LANES
    rows = N_KEYS * PEER_HEADS
    return pl.pallas_call(
        functools.partial(_select_kernel, tt), name="select",
        grid=(T // tt,),
        in_specs=[pl.BlockSpec((tt, D), lambda i: (i, 0)),
                  pl.BlockSpec(ws.shape, lambda i: (0, 0))],
        out_specs=[pl.BlockSpec((nlt, rows, LANES), lambda i: (i, 0, 0)),
                   pl.BlockSpec((nlt, rows, LANES), lambda i: (i, 0, 0)),
                   pl.BlockSpec((PEER_HEADS, nlt, N_KEYS, LANES), lambda i: (0, i, 0, 0)),
                   pl.BlockSpec((PEER_HEADS, nlt, N_KEYS, LANES), lambda i: (0, i, 0, 0))],
        out_shape=[jax.ShapeDtypeStruct((T // LANES, rows, LANES), F32),
                   jax.ShapeDtypeStruct((T // LANES, rows, LANES), F32),
                   jax.ShapeDtypeStruct((PEER_HEADS, T // LANES, N_KEYS, LANES), BF16),
                   jax.ShapeDtypeStruct((PEER_HEADS, T // LANES, N_KEYS, LANES), BF16)],
        scratch_shapes=[pltpu.VMEM((nlt, 2 * PEER_HEADS * N_KEYS, LANES), F32),
                        pltpu.VMEM((PEER_HEADS, nlt, N_KEYS, LANES), F32),
                        pltpu.VMEM((2, PEER_TOPK, nlt, PEER_HEADS, LANES), F32),
                        pltpu.VMEM((PEER_TOPK, nlt, PEER_HEADS, LANES), F32),
                        pltpu.VMEM((nlt, PEER_HEADS, LANES), F32)],
        compiler_params=_cparams(1),
    )(h2, ws)


def _gelu_tanh_sigmoid_form(x):
    k0 = -2.0 * 0.7978845608028654 * 1.4426950408889634
    k1 = k0 * 0.044715
    return x * (1.0 / (1.0 + jnp.exp2(x * (k1 * (x * x) + k0))))


DENSE_CW = 2 * LANES
DENSE_RS = 2 * LANES


def _dense_kernel(eb, final_norm, h2_ref, u_ref, vT_ref, e1_ref, n1_ref, e2_ref, r2_ref,
                  x1_ref, gt_ref, fg_ref, o_ref, a_scr, g_scr, acc_scr):
    j = pl.program_id(2)
    nchunk, d_model, cw = acc_scr.shape
    lpc = cw // LANES
    rs = DENSE_RS
    nsl = eb // rs
    assert d_model // rs == nsl

    @pl.when(j == 0)
    def _():
        acc_scr[...] = jnp.zeros_like(acc_scr)

    def slice_tiles(i):
        return [(k, l) for k in range(i * rs // N_KEYS, (i + 1) * rs // N_KEYS) for l in range(lpc)]

    def score_slice(c, i):
        a = lax.dot_general(u_ref[i * rs:(i + 1) * rs, :], h2_ref[c * cw:(c + 1) * cw, :], NT_DIMS,
                            preferred_element_type=F32)
        a_scr[c % 2, i * rs:(i + 1) * rs, :] = _gelu_tanh_sigmoid_form(a.astype(BF16))

    def weight_slice(c, i):
        for k, l in slice_tiles(i):
            rows = slice(k * N_KEYS, (k + 1) * N_KEYS)
            heads = slice(k * PEER_HEADS, (k + 1) * PEER_HEADS)
            lanes = slice(l * LANES, (l + 1) * LANES)
            lt = c * lpc + l
            e1_all = e1_ref[lt, heads, :]
            n1_all = n1_ref[lt, heads, :]
            gate = None
            for h in range(PEER_HEADS):
                e1 = jnp.broadcast_to(e1_all[h:h + 1, :], (N_KEYS, LANES)).astype(BF16)
                n1 = jnp.broadcast_to(n1_all[h:h + 1, :], (N_KEYS, LANES)).astype(BF16)
                term = jnp.minimum(e1 * e2_ref[h, lt], jnp.maximum(n1 - r2_ref[h, lt], jnp.zeros((), BF16)))
                gate = term if gate is None else gate + term
            g_scr[c % 2, rows, lanes] = gate * a_scr[c % 2, rows, lanes]

    def output_slice(c, i):
        acc_scr[c, i * rs:(i + 1) * rs, :] += jnp.dot(vT_ref[i * rs:(i + 1) * rs, :], g_scr[c % 2],
                                                      preferred_element_type=F32)

    for c in range(nchunk + 1):
        for i in range(nsl):
            if c < nchunk:
                score_slice(c, i)
            if c > 0:
                output_slice(c - 1, i)
            if c < nchunk and i > 0:
                weight_slice(c, i - 1)
        if c < nchunk:
            weight_slice(c, nsl - 1)

    @pl.when(j == pl.num_programs(2) - 1)
    def _():
        for c in range(nchunk):
            tok = slice(c * cw, (c + 1) * cw)
            x2 = x1_ref[tok, :] + gt_ref[...] * acc_scr[c].T
            if final_norm:
                x2 = x2 * lax.rsqrt(jnp.mean(x2 * x2, axis=-1, keepdims=True) + EPS) * fg_ref[...]
            o_ref[tok, :] = x2


def _dense(h2, u_bf, vT_bf, e1n, n1, e2, r2, x1, gt2, fg, final_norm, tt, eb):
    B, S, D = x1.shape
    E = u_bf.shape[0]
    nt = S // tt
    rb = eb // N_KEYS * PEER_HEADS
    cw = min(tt, DENSE_CW)
    assert eb == D and eb % DENSE_RS == 0
    gate_spec = pl.BlockSpec((tt // LANES, rb, LANES), lambda b, i, j: (b * nt + i, j, 0))
    tile_spec = pl.BlockSpec((PEER_HEADS, tt // LANES, N_KEYS, LANES), lambda b, i, j: (0, b * nt + i, 0, 0))
    return pl.pallas_call(
        functools.partial(_dense_kernel, eb, final_norm), name="dense",
        grid=(B, nt, E // eb),
        in_specs=[pl.BlockSpec((None, tt, D), lambda b, i, j: (b, i, 0)),
                  pl.BlockSpec((eb, D), lambda b, i, j: (j, 0)),
                  pl.BlockSpec((D, eb), lambda b, i, j: (0, j)),
                  gate_spec, gate_spec, tile_spec, tile_spec,
                  pl.BlockSpec((None, tt, D), lambda b, i, j: (b, i, 0)),
                  pl.BlockSpec((None, 1, D), lambda b, i, j: (b, 0, 0)),
                  pl.BlockSpec((1, D), lambda b, i, j: (0, 0))],
        out_specs=pl.BlockSpec((None, tt, D), lambda b, i, j: (b, i, 0)),
        out_shape=jax.ShapeDtypeStruct((B, S, D), F32),
        scratch_shapes=[pltpu.VMEM((2, eb, cw), BF16), pltpu.VMEM((2, eb, cw), BF16),
                        pltpu.VMEM((tt // cw, D, cw), F32)],
        compiler_params=_cparams(3),
    )(h2, u_bf, vT_bf, e1n, n1, e2, r2, x1, gt2, fg)


def _tile(n, pref):
    return pref if n % pref == 0 else n


def kernel(x, c, w_ada, b_ada, norm1_g, w_in, conv_w, conv_b, b_igate, b_fgate, rel_bias, mlstm_norm_g,
           w_branch_a, w_branch_b, w_out, norm2_g, w_peer_q, peer_sub_keys, peer_u, peer_v, final_g):
    B, S, D = x.shape
    depth = w_ada.shape[0]
    for l in range(depth):
        mod = _mod(c, w_ada[l], b_ada[l])
        sh1, sc1, gt1, sh2, sc2, gt2 = [m.reshape(B, 1, D) for m in jnp.split(mod, 6, axis=-1)]

        o = [0]
        for sz in (A_WIDTH, A_WIDTH, A_WIDTH, 2 * B_WIDTH, B_WIDTH, B_WIDTH, B_HEADS, B_HEADS, D, D):
            o.append(o[-1] + sz)
        wl = w_in[l]
        seg = lambda i: wl[:, o[i]:o[i + 1]]
        w_main = jnp.concatenate([seg(8), seg(9), seg(0), seg(1), seg(2), seg(3), seg(4), seg(5)], axis=1).astype(BF16)
        w_gate = jnp.concatenate([seg(6), seg(7), jnp.zeros((D, LANES - 2 * B_HEADS), F32)], axis=1).astype(BF16)
        w_gateT = w_gate[:, :SUBLANES].T
        gb = jnp.concatenate([b_igate[l], b_fgate[l]]).astype(F32)
        tm = _tile(S, 512)
        gbias = jnp.concatenate([gb, jnp.zeros((LANES - 2 * B_HEADS,), F32)]).reshape(1, LANES)
        gbiasT = jnp.broadcast_to(gb[:, None], (SUBLANES, tm))
        proj, gates, gatesT = _inproj(x, sh1, sc1, norm1_g[l].reshape(1, D), w_main, w_gate, w_gateT,
                                      gbias, gbiasT, tm, P_COLS)

        tq = _tile(S, 256)
        y_a = _attention(proj, rel_bias[l], tq, 2 if (S // tq) % 2 == 0 else 1)
        y_b = _mlstm(proj, gates, gatesT, conv_w[l], conv_b[l], mlstm_norm_g[l], _tile(S, 128), _tile(B, 2))
        x1, h2 = _merge(y_a, y_b, proj, x, gt1, w_branch_a[l].astype(BF16), w_branch_b[l].astype(BF16),
                        w_out[l].astype(BF16), norm2_g[l].reshape(1, D), sh2, sc2, _tile(S, 512))

        ws = _fold(w_peer_q[l], peer_sub_keys[l])
        e1n, n1, e2, r2 = _select(h2.reshape(B * S, D), ws, _tile(B * S, 512))
        x = _dense(h2, peer_u[l].astype(BF16), peer_v[l].T.astype(BF16), e1n, n1, e2, r2, x1, gt2,
                   final_g.reshape(1, D), l == depth - 1, _tile(S, 1024), 1024)
    return x
```

```python
import functools

import jax
import jax.numpy as jnp
from jax import lax
from jax.experimental import pallas as pl
from jax.experimental.pallas import tpu as pltpu

F32 = jnp.float32
BF16 = jnp.bfloat16

CHUNK = 64
LEFT_CHUNKS = 8
A_HEADS = 8
A_HEAD_DIM = 64
A_WIDTH = A_HEADS * A_HEAD_DIM
REL_CLIP = 128
B_HEADS = 4
B_HEAD_DIM = 128
B_WIDTH = B_HEADS * B_HEAD_DIM
CONV_K = 4
PEER_HEADS = 8
N_KEYS = 128
PEER_TOPK = 16
D_KEY = 256
EPS = 1e-6

LANES = 128
SUBLANES = 8
NEG = -1e30
VMEM_LIMIT = 56 * 1024 * 1024

PB_GA, PB_GB, PB_QA, PB_KA, PB_VA, PB_QP, PB_KP, PB_VB, PB_OB = 0, 2, 4, 5, 6, 7, 8, 9, 10
P_COLS = 11 * 512

NT_DIMS = (((1,), (1,)), ((), ()))
TN_DIMS = (((0,), (0,)), ((), ()))


def _cparams(n_axes, flags=None):
    return pltpu.CompilerParams(dimension_semantics=("arbitrary",) * n_axes, vmem_limit_bytes=VMEM_LIMIT,
                                flags=flags)


def _sigmoid(x):
    return 1.0 / (1.0 + jnp.exp(-x))


def _log_sigmoid(x):
    return jnp.minimum(x, 0.0) - jnp.log(1.0 + jnp.exp(-jnp.abs(x)))


def _mod_kernel(c_ref, w_ref, b_ref, o_ref):
    c = c_ref[...]
    o_ref[...] = jnp.dot(c * _sigmoid(c), w_ref[...], preferred_element_type=F32,
                         precision=lax.Precision.HIGHEST) + b_ref[...]


def _mod(c, w, b):
    B, D = c.shape
    N = w.shape[1]
    tn = 512
    return pl.pallas_call(
        _mod_kernel, name="mod",
        grid=(N // tn,),
        in_specs=[pl.BlockSpec((B, D), lambda j: (0, 0)),
                  pl.BlockSpec((D, tn), lambda j: (0, j)),
                  pl.BlockSpec((1, tn), lambda j: (0, j))],
        out_specs=pl.BlockSpec((B, tn), lambda j: (0, j)),
        out_shape=jax.ShapeDtypeStruct((B, N), F32),
        compiler_params=_cparams(1),
    )(c, w, b.reshape(1, N))


def _inproj_kernel(x_ref, sh_ref, sc_ref, g_ref, w_ref, wg_ref, wgT_ref, gb_ref, gbT_ref,
                   o_ref, og_ref, ogT_ref, h_scr):
    @pl.when(pl.program_id(2) == 0)
    def _():
        x = x_ref[...]
        y = x * lax.rsqrt(jnp.mean(x * x, axis=-1, keepdims=True) + EPS) * g_ref[...]
        hb = (y * (1.0 + sc_ref[...]) + sh_ref[...]).astype(BF16)
        h_scr[...] = hb
        og_ref[...] = jnp.dot(hb, wg_ref[...], preferred_element_type=F32) + gb_ref[...]
        ogT_ref[...] = lax.dot_general(wgT_ref[...], hb, NT_DIMS, preferred_element_type=F32) + gbT_ref[...]

    o_ref[...] = jnp.dot(h_scr[...], w_ref[...], preferred_element_type=F32).astype(o_ref.dtype)


def _inproj(x, sh, sc, g, w_main, w_gate, w_gateT, gbias, gbiasT, tm, tn):
    B, S, D = x.shape
    N = w_main.shape[1]
    row = lambda b, i, j: (b, 0, 0)
    return pl.pallas_call(
        _inproj_kernel, name="inproj",
        grid=(B, S // tm, N // tn),
        in_specs=[pl.BlockSpec((None, tm, D), lambda b, i, j: (b, i, 0)),
                  pl.BlockSpec((None, 1, D), row),
                  pl.BlockSpec((None, 1, D), row),
                  pl.BlockSpec((1, D), lambda b, i, j: (0, 0)),
                  pl.BlockSpec((D, tn), lambda b, i, j: (0, j)),
                  pl.BlockSpec((D, LANES), lambda b, i, j: (0, 0)),
                  pl.BlockSpec((SUBLANES, D), lambda b, i, j: (0, 0)),
                  pl.BlockSpec((1, LANES), lambda b, i, j: (0, 0)),
                  pl.BlockSpec((SUBLANES, tm), lambda b, i, j: (0, 0))],
        out_specs=[pl.BlockSpec((None, tm, tn), lambda b, i, j: (b, i, j)),
                   pl.BlockSpec((None, tm, LANES), lambda b, i, j: (b, i, 0)),
                   pl.BlockSpec((None, SUBLANES, tm), lambda b, i, j: (b, 0, i))],
        out_shape=[jax.ShapeDtypeStruct((B, S, N), BF16),
                   jax.ShapeDtypeStruct((B, S, LANES), F32),
                   jax.ShapeDtypeStruct((B, SUBLANES, S), F32)],
        scratch_shapes=[pltpu.VMEM((tm, D), BF16)],
        compiler_params=_cparams(3),
    )(x, sh, sc, g, w_main, w_gate, w_gateT, gbias, gbiasT)


def _attn_kernel(nprev, nq, q_ref, *refs):
    nkv = nprev + nq
    k_refs = refs[:nkv]
    v_refs = refs[nkv:2 * nkv]
    bias_ref, o_ref = refs[2 * nkv:]
    tq = q_ref.shape[0] // nq
    i = pl.program_id(2)
    lo = lax.broadcasted_iota(jnp.int32, (1, LANES), 1) < A_HEAD_DIM
    scale = A_HEAD_DIM ** -0.5
    probs = [(qq, hh) for qq in range(nq) for hh in range(2)]
    qm = []
    for qq, hh in probs:
        q = q_ref[qq * tq:(qq + 1) * tq, :]
        qm.append(jnp.where(lo if hh == 0 else jnp.logical_not(lo), q, jnp.zeros_like(q)))
    ss = [[] for _ in probs]
    for c in range(nprev + 1):
        for n, (qq, hh) in enumerate(probs):
            s = lax.dot_general(qm[n], k_refs[qq + c][...], NT_DIMS, preferred_element_type=F32) * scale
            s = s + bias_ref[hh, c]
            if c < nprev:
                s = jnp.where(i * nq + qq >= nprev - c, s, NEG)
            ss[n].append(s)
    m = [functools.reduce(jnp.maximum, [jnp.max(s, axis=-1, keepdims=True) for s in ss[n]]) for n in range(len(probs))]
    l = [None for _ in probs]
    acc = [None for _ in probs]
    for c in range(nprev + 1):
        for n, (qq, hh) in enumerate(probs):
            p = jnp.exp(ss[n][c] - m[n])
            pl_ = jnp.sum(p, axis=-1, keepdims=True)
            pv = jnp.dot(p.astype(BF16), v_refs[qq + c][...], preferred_element_type=F32)
            l[n] = pl_ if l[n] is None else l[n] + pl_
            acc[n] = pv if acc[n] is None else acc[n] + pv
    for qq in range(nq):
        a0, a1 = 2 * qq, 2 * qq + 1
        o_ref[qq * tq:(qq + 1) * tq, :] = jnp.where(lo, acc[a0] / l[a0], acc[a1] / l[a1]).astype(o_ref.dtype)


def _attn_bias(rel_bias, tq):
    nprev = LEFT_CHUNKS * CHUNK // tq
    nk = (nprev + 1) * tq
    ql = jnp.arange(tq)[:, None]
    kl = jnp.arange(nk)[None, :] - nprev * tq
    L = nk + tq
    delta = jnp.arange(L)
    delta = jnp.where(delta < nk, delta, delta - L)
    w = rel_bias[:, jnp.clip(nprev * tq - delta, -REL_CLIP, REL_CLIP) + REL_CLIP].astype(F32)
    toep = jnp.tile(w, (1, tq))[:, :tq * (L - 1)].reshape(A_HEADS, tq, L - 1)[:, :, :nk]
    qc = ql // CHUNK
    kc = jnp.floor_divide(kl, CHUNK)
    vis = (kc <= qc) & (kc >= qc - LEFT_CHUNKS)
    bias = jnp.where(vis[None], toep, NEG)
    return bias.reshape(A_HEADS, tq, nprev + 1, tq).transpose(0, 2, 1, 3)


def _attention(proj, rel_bias, tq, nq):
    B, S, _ = proj.shape
    nprev = LEFT_CHUNKS * CHUNK // tq
    nkv = nprev + nq
    bias = _attn_bias(rel_bias, tq)
    npair = A_HEADS // 2
    qb, kb, vb = PB_QA * 4, PB_KA * 4, PB_VA * 4

    def kv_spec(base, c):
        back = nprev - c
        return pl.BlockSpec((None, tq, LANES), lambda p, b, i: (b, jnp.maximum(i * nq - back, 0), base + p))

    in_specs = [pl.BlockSpec((None, nq * tq, LANES), lambda p, b, i: (b, i, qb + p))]
    in_specs += [kv_spec(kb, c) for c in range(nkv)]
    in_specs += [kv_spec(vb, c) for c in range(nkv)]
    in_specs += [pl.BlockSpec((2, nprev + 1, tq, tq), lambda p, b, i: (p, 0, 0, 0))]
    return pl.pallas_call(
        functools.partial(_attn_kernel, nprev, nq), name="attn",
        grid=(npair, B, S // (nq * tq)),
        in_specs=in_specs,
        out_specs=pl.BlockSpec((None, nq * tq, LANES), lambda p, b, i: (b, i, p)),
        out_shape=jax.ShapeDtypeStruct((B, S, A_WIDTH), BF16),
        compiler_params=_cparams(3),
    )(*([proj] * (2 * nkv + 1)), bias)


MLSTM_HEAD_GROUP = 4


def _mlstm_kernel(nb, rows, qp_ref, kp_ref, v_ref, ob_ref, g_ref, gT_ref, cw_ref, cb_ref, ng_ref, tril_ref, triu_ref,
                  o_ref, C_scr, n_scr, m_scr, ext_scr):
    @pl.when(pl.program_id(1) == 0)
    def _():
        C_scr[...] = jnp.zeros_like(C_scr)
        n_scr[...] = jnp.zeros_like(n_scr)
        m_scr[...] = jnp.zeros_like(m_scr)
        ext_scr[:, :, 0:SUBLANES, :] = jnp.zeros((nb, 2, SUBLANES, B_WIDTH), F32)

    q_all, k_all = [], []
    for bb in range(nb):
        qk = []
        for t, ref in enumerate((qp_ref, kp_ref)):
            ext_scr[bb, t, SUBLANES:SUBLANES + rows, :] = ref[bb].astype(F32)
            acc = cb_ref[:, t * B_WIDTH:(t + 1) * B_WIDTH]
            for kk in range(CONV_K):
                start = SUBLANES - (CONV_K - 1) + kk
                acc = acc + cw_ref[kk:kk + 1, t * B_WIDTH:(t + 1) * B_WIDTH] * ext_scr[bb, t, start:start + rows, :]
            ext_scr[bb, t, 0:SUBLANES, :] = ext_scr[bb, t, rows:rows + SUBLANES, :]
            qk.append(acc * _sigmoid(acc))
        q_all.append(qk[0])
        k_all.append(qk[1] * (B_HEAD_DIM ** -0.5))

    causal = (lax.broadcasted_iota(jnp.int32, (CHUNK, CHUNK), 0) >= lax.broadcasted_iota(jnp.int32, (CHUNK, CHUNK), 1))
    hi = lax.Precision.HIGHEST
    each = lambda f, *cols: [f(*a) for a in zip(*cols)]

    for c in range(rows // CHUNK):
        r0 = c * CHUNK
        tsl = slice(r0, r0 + CHUNK)
        for hg in range(0, B_HEADS, MLSTM_HEAD_GROUP):
            chains = [(bb, h) for h in range(hg, hg + MLSTM_HEAD_GROUP) for bb in range(nb)]
            hsl = [slice(h * B_HEAD_DIM, (h + 1) * B_HEAD_DIM) for _, h in chains]
            slot = [bb * B_HEADS + h for bb, h in chains]
            qh = [q_all[bb][tsl, sl] for (bb, _), sl in zip(chains, hsl)]
            kh = [k_all[bb][tsl, sl] for (bb, _), sl in zip(chains, hsl)]
            vh = [v_ref[bb, tsl, sl] for (bb, _), sl in zip(chains, hsl)]
            qh_b = each(lambda x: x.astype(BF16), qh)
            li_b = [jnp.broadcast_to(g_ref[bb, tsl, h:h + 1], (CHUNK, LANES)) for bb, h in chains]
            lf_b = [_log_sigmoid(jnp.broadcast_to(g_ref[bb, tsl, B_HEADS + h:B_HEADS + h + 1], (CHUNK, LANES)))
                    for bb, h in chains]
            li_r = [jnp.broadcast_to(gT_ref[bb, h:h + 1, tsl], (CHUNK, CHUNK)) for bb, h in chains]
            lf_r = [_log_sigmoid(jnp.broadcast_to(gT_ref[bb, B_HEADS + h:B_HEADS + h + 1, tsl], (CHUNK, CHUNK)))
                    for bb, h in chains]
            a_b = each(lambda x: jnp.dot(tril_ref[...], x, preferred_element_type=F32, precision=hi), lf_b)
            a_r = each(lambda x: jnp.dot(x, triu_ref[...], preferred_element_type=F32, precision=hi), lf_r)
            D = each(lambda ab, ar, lr: jnp.where(causal, ab[:, :CHUNK] - ar + lr, NEG), a_b, a_r, li_r)

            m_prev = [m_scr[s, 0:1, :] for s in slot]
            inter = each(lambda ab, mp: ab + mp, a_b, m_prev)
            m_row = each(lambda it, d: jnp.maximum(it, jnp.max(d, axis=-1, keepdims=True)), inter, D)
            w_inter = each(lambda it, mr: jnp.exp(it - mr), inter, m_row)
            Wd = each(lambda d, mr: jnp.exp(d - mr[:, :CHUNK]), D, m_row)
            qkw = each(lambda q, k, w: lax.dot_general(q, k.astype(BF16), NT_DIMS, preferred_element_type=F32) * w,
                       qh_b, kh, Wd)
            qC = [jnp.dot(q, C_scr[s].astype(BF16), preferred_element_type=F32) for q, s in zip(qh_b, slot)]
            qkv = each(lambda p, v: jnp.dot(p.astype(BF16), v, preferred_element_type=F32), qkw, vh)
            num = each(lambda w, a, b: w * a + b, w_inter, qC, qkv)
            qn = [jnp.sum(q * n_scr[s, 0:1, :], axis=-1, keepdims=True) for q, s in zip(qh, slot)]
            den = each(lambda w, a, p: w[:, 0:1] * a + jnp.sum(p, axis=-1, keepdims=True), w_inter, qn, qkw)
            hh = each(lambda nu, de, mr: nu / jnp.maximum(jnp.abs(de), jnp.exp(-mr[:, 0:1])), num, den, m_row)

            A_row = each(lambda ab: ab[CHUNK - 1:CHUNK, :], a_b)
            g_b = each(lambda A, ab, lb: A - ab + lb, A_row, a_b, li_b)
            m_new = each(lambda A, mp, g: jnp.maximum(A + mp, jnp.max(g, axis=0, keepdims=True)), A_row, m_prev, g_b)
            decay = each(lambda A, mp, mn: jnp.exp(A + mp - mn), A_row, m_prev, m_new)
            kw = each(lambda g, mn, k: jnp.exp(g - mn) * k, g_b, m_new, kh)
            dC = each(lambda w, v: lax.dot_general(w.astype(BF16), v, TN_DIMS, preferred_element_type=F32), kw, vh)
            for s, de, d, w, mn in zip(slot, decay, dC, kw, m_new):
                C_scr[s] = de * C_scr[s] + d
                n_scr[s] = jnp.broadcast_to(de * n_scr[s, 0:1, :] + jnp.sum(w, axis=0, keepdims=True),
                                            (SUBLANES, LANES))
                m_scr[s] = jnp.broadcast_to(mn, (SUBLANES, LANES))

            for (bb, _), sl, x in zip(chains, hsl, hh):
                hn = x * lax.rsqrt(jnp.mean(x * x, axis=-1, keepdims=True) + EPS) * ng_ref[:, sl]
                ob = ob_ref[bb, tsl, sl].astype(F32)
                o_ref[bb, tsl, sl] = (_sigmoid(ob) * hn).astype(o_ref.dtype)


def _mlstm(proj, gates, gatesT, conv_w, conv_b, norm_g, rows, nb):
    B, S, _ = proj.shape
    tril = jnp.tril(jnp.ones((CHUNK, CHUNK), F32))
    blk = lambda cb: pl.BlockSpec((nb, rows, B_WIDTH), lambda b, j: (b, j, cb))
    full = lambda shape: pl.BlockSpec(shape, lambda b, j: (0,) * len(shape))
    return pl.pallas_call(
        functools.partial(_mlstm_kernel, nb, rows), name="mlstm",
        grid=(B // nb, S // rows),
        in_specs=[blk(PB_QP), blk(PB_KP), blk(PB_VB), blk(PB_OB),
                  pl.BlockSpec((nb, rows, LANES), lambda b, j: (b, j, 0)),
                  pl.BlockSpec((nb, SUBLANES, rows), lambda b, j: (b, 0, j)),
                  full((CONV_K, 2 * B_WIDTH)), full((1, 2 * B_WIDTH)), full((1, B_WIDTH)),
                  full((CHUNK, CHUNK)), full((CHUNK, CHUNK))],
        out_specs=pl.BlockSpec((nb, rows, B_WIDTH), lambda b, j: (b, j, 0)),
        out_shape=jax.ShapeDtypeStruct((B, S, B_WIDTH), BF16),
        scratch_shapes=[pltpu.VMEM((nb * B_HEADS, B_HEAD_DIM, B_HEAD_DIM), F32),
                        pltpu.VMEM((nb * B_HEADS, SUBLANES, LANES), F32),
                        pltpu.VMEM((nb * B_HEADS, SUBLANES, LANES), F32),
                        pltpu.VMEM((nb, 2, rows + SUBLANES, B_WIDTH), F32)],
        compiler_params=_cparams(2),
    )(proj, proj, proj, proj, gates, gatesT, conv_w, conv_b.reshape(1, -1), norm_g.reshape(1, -1), tril, tril.T)


def _merge_kernel(ya_ref, yb_ref, ga_ref, gb_ref, x_ref, gt_ref, wa_ref, wb_ref, wo_ref, n2_ref, sh_ref, sc_ref,
                  x1_ref, h2_ref):
    ya = jnp.dot(ya_ref[...], wa_ref[...], preferred_element_type=F32)
    yb = jnp.dot(yb_ref[...], wb_ref[...], preferred_element_type=F32)
    y = _sigmoid(ga_ref[...].astype(F32)) * ya + _sigmoid(gb_ref[...].astype(F32)) * yb
    x1 = x_ref[...] + gt_ref[...] * jnp.dot(y.astype(BF16), wo_ref[...], preferred_element_type=F32)
    x1_ref[...] = x1
    yn = x1 * lax.rsqrt(jnp.mean(x1 * x1, axis=-1, keepdims=True) + EPS) * n2_ref[...]
    h2_ref[...] = (yn * (1.0 + sc_ref[...]) + sh_ref[...]).astype(h2_ref.dtype)


def _merge(y_a, y_b, proj, x, gt1, wa, wb, wo, n2g, sh2, sc2, tm):
    B, S, D = x.shape
    row = pl.BlockSpec((None, 1, D), lambda b, i: (b, 0, 0))
    full = lambda shape: pl.BlockSpec(shape, lambda b, i: (0,) * len(shape))
    return pl.pallas_call(
        _merge_kernel, name="merge",
        grid=(B, S // tm),
        in_specs=[pl.BlockSpec((None, tm, A_WIDTH), lambda b, i: (b, i, 0)),
                  pl.BlockSpec((None, tm, B_WIDTH), lambda b, i: (b, i, 0)),
                  pl.BlockSpec((None, tm, D), lambda b, i: (b, i, PB_GA // 2)),
                  pl.BlockSpec((None, tm, D), lambda b, i: (b, i, PB_GB // 2)),
                  pl.BlockSpec((None, tm, D), lambda b, i: (b, i, 0)),
                  row, full(wa.shape), full(wb.shape), full(wo.shape), full((1, D)), row, row],
        out_specs=[pl.BlockSpec((None, tm, D), lambda b, i: (b, i, 0)),
                   pl.BlockSpec((None, tm, D), lambda b, i: (b, i, 0))],
        out_shape=[jax.ShapeDtypeStruct((B, S, D), F32), jax.ShapeDtypeStruct((B, S, D), BF16)],
        compiler_params=_cparams(2),
    )(y_a, y_b, proj, proj, x, gt1, wa, wb, wo, n2g, sh2, sc2)


def _fold_kernel(sk_ref, wq_ref, o_ref):
    o_ref[...] = lax.dot_general(sk_ref[...], wq_ref[...], NT_DIMS, preferred_element_type=F32,
                                 precision=lax.Precision.HIGHEST).astype(o_ref.dtype)


def _fold(w_pq, sub_keys):
    D = w_pq.shape[0]
    dh = D_KEY // 2
    nhp = PEER_HEADS * 2
    return pl.pallas_call(
        _fold_kernel, name="fold",
        grid=(nhp,),
        in_specs=[pl.BlockSpec((None, N_KEYS, dh), lambda j: (j, 0, 0)),
                  pl.BlockSpec((D, dh), lambda j: (0, j))],
        out_specs=pl.BlockSpec((N_KEYS, D), lambda j: (j, 0)),
        out_shape=jax.ShapeDtypeStruct((nhp * N_KEYS, D), BF16),
        compiler_params=_cparams(1),
    )(sub_keys.reshape(nhp, N_KEYS, dh), w_pq)


_CELLS = [(r1, r2) for r1 in range(PEER_TOPK) for r2 in range(PEER_TOPK) if (r1 + 1) * (r2 + 1) <= PEER_TOPK]


def _batcher_pairs(lo, hi):
    def merge(lo, hi, r):
        step = r * 2
        if step < hi - lo:
            yield from merge(lo, hi, step)
            yield from merge(lo + r, hi, step)
            yield from [(i, i + r) for i in range(lo + r, hi - r, step)]
        else:
            yield (lo, lo + r)

    if hi - lo >= 1:
        mid = lo + (hi - lo) // 2
        yield from _batcher_pairs(lo, mid)
        yield from _batcher_pairs(mid + 1, hi)
        yield from merge(lo, hi, 1)


_KEY_GROUPS = N_KEYS // SUBLANES
_SORT_NET = list(_batcher_pairs(0, _KEY_GROUPS - 1))


def _select_kernel(tt, h2_ref, ws_ref, e1_ref, n1_ref, e2_ref, r2_ref,
                   s_scr, e1_scr, v_scr, n_scr, z_scr):
    nlt = tt // LANES
    s = lax.dot_general(ws_ref[...], h2_ref[...], NT_DIMS, preferred_element_type=F32)
    for lt in range(nlt):
        s_scr[lt] = s[:, lt * LANES:(lt + 1) * LANES]

    def extract(it, carry):
        h = it // nlt
        lt = it % nlt
        for p in range(2):
            sv = s_scr[lt, pl.ds(pl.multiple_of((h * 2 + p) * N_KEYS, N_KEYS), N_KEYS), :]
            cols = [sv[i * SUBLANES:(i + 1) * SUBLANES, :] for i in range(_KEY_GROUPS)]
            for i, j in _SORT_NET:
                cols[i], cols[j] = jnp.maximum(cols[i], cols[j]), jnp.minimum(cols[i], cols[j])
            tops = []
            for r in range(PEER_TOPK):
                m = jnp.max(cols[0], axis=0, keepdims=True)
                tops.append(m)
                v_scr[p, r, lt, pl.ds(h, 1), :] = m
                live = PEER_TOPK - 1 - r
                if live > 0:
                    eq = cols[0] == m
                    for i in range(live):
                        cols[i] = jnp.where(eq, cols[i + 1], cols[i])
            e = jnp.exp(sv - tops[0])
            if p == 0:
                e1_scr[h, lt] = e
            else:
                rank = jnp.full((N_KEYS, LANES), float(PEER_TOPK), F32)
                for r in reversed(range(PEER_TOPK)):
                    rank = jnp.where(sv >= tops[r], float(r), rank)
                r2_ref[h, lt] = rank.astype(r2_ref.dtype)
                e2_ref[h, lt] = e.astype(e2_ref.dtype)
        return carry

    lax.fori_loop(0, PEER_HEADS * nlt, extract, 0)

    def cells(lt, carry):
        v1 = [v_scr[0, r, lt] for r in range(PEER_TOPK)]
        v2 = [v_scr[1, r, lt] for r in range(PEER_TOPK)]
        vals = [v1[r1] + v2[r2] for (r1, r2) in _CELLS]
        nc = len(_CELLS)
        static = [0] * nc
        lost = [None] * nc
        won = [None] * nc
        for a in range(nc):
            for b in range(a + 1, nc):
                (a1, a2), (b1, b2) = _CELLS[a], _CELLS[b]
                if a1 <= b1 and a2 <= b2:
                    static[b] += 1
                else:
                    inc = jnp.where(vals[a] >= vals[b], 1.0, 0.0)
                    lost[b] = inc if lost[b] is None else lost[b] + inc
                    won[a] = inc if won[a] is None else won[a] + inc
                    static[a] += 1
        cnt = [None] * PEER_TOPK
        z = None
        for ci, (r1, r2) in enumerate(_CELLS):
            rk = float(static[ci])
            if lost[ci] is not None:
                rk = rk + lost[ci]
            if won[ci] is not None:
                rk = rk - won[ci]
            sel = jnp.where(rk < float(PEER_TOPK), 1.0, 0.0)
            cnt[r1] = sel if cnt[r1] is None else cnt[r1] + sel
            zt = sel * jnp.exp(vals[ci] - vals[0])
            z = zt if z is None else z + zt
        for r in range(PEER_TOPK):
            n_scr[r, lt] = cnt[r]
        z_scr[lt] = 1.0 / z
        return carry

    lax.fori_loop(0, nlt, cells, 0)

    for h in range(PEER_HEADS):
        for lt in range(nlt):
            s1 = s_scr[lt, 2 * h * N_KEYS:(2 * h + 1) * N_KEYS, :]
            n1 = jnp.zeros((N_KEYS, LANES), F32)
            for r in reversed(range(PEER_TOPK)):
                n1 = jnp.where(s1 >= v_scr[0, r, lt, h:h + 1, :], n_scr[r, lt, h:h + 1, :], n1)
            e1 = e1_scr[h, lt] * z_scr[lt, h:h + 1, :]
            n1_ref[lt, pl.ds(h, N_KEYS, stride=PEER_HEADS), :] = n1
            e1_ref[lt, pl.ds(h, N_KEYS, stride=PEER_HEADS), :] = e1


def _select(h2, ws, tt):
    T, D = h2.shape
    nlt = tt // LANES
    rows = N_KEYS * PEER_HEADS
    return pl.pallas_call(
        functools.partial(_select_kernel, tt), name="select",
        grid=(T // tt,),
        in_specs=[pl.BlockSpec((tt, D), lambda i: (i, 0)),
                  pl.BlockSpec(ws.shape, lambda i: (0, 0))],
        out_specs=[pl.BlockSpec((nlt, rows, LANES), lambda i: (i, 0, 0)),
                   pl.BlockSpec((nlt, rows, LANES), lambda i: (i, 0, 0)),
                   pl.BlockSpec((PEER_HEADS, nlt, N_KEYS, LANES), lambda i: (0, i, 0, 0)),
                   pl.BlockSpec((PEER_HEADS, nlt, N_KEYS, LANES), lambda i: (0, i, 0, 0))],
        out_shape=[jax.ShapeDtypeStruct((T // LANES, rows, LANES), F32),
                   jax.ShapeDtypeStruct((T // LANES, rows, LANES), F32),
                   jax.ShapeDtypeStruct((PEER_HEADS, T // LANES, N_KEYS, LANES), BF16),
                   jax.ShapeDtypeStruct((PEER_HEADS, T // LANES, N_KEYS, LANES), BF16)],
        scratch_shapes=[pltpu.VMEM((nlt, 2 * PEER_HEADS * N_KEYS, LANES), F32),
                        pltpu.VMEM((PEER_HEADS, nlt, N_KEYS, LANES), F32),
                        pltpu.VMEM((2, PEER_TOPK, nlt, PEER_HEADS, LANES), F32),
                        pltpu.VMEM((PEER_TOPK, nlt, PEER_HEADS, LANES), F32),
                        pltpu.VMEM((nlt, PEER_HEADS, LANES), F32)],
        compiler_params=_cparams(1),
    )(h2, ws)


def _gelu_tanh_sigmoid_form(x):
    k0 = -2.0 * 0.7978845608028654 * 1.4426950408889634
    k1 = k0 * 0.044715
    return x * (1.0 / (1.0 + jnp.exp2(x * (k1 * (x * x) + k0))))


DENSE_CW = 2 * LANES
DENSE_RS = 2 * LANES


def _dense_kernel(eb, final_norm, h2_ref, u_ref, vT_ref, e1_ref, n1_ref, e2_ref, r2_ref,
                  x1_ref, gt_ref, fg_ref, o_ref, a_scr, g_scr, acc_scr):
    j = pl.program_id(2)
    nchunk, d_model, cw = acc_scr.shape
    lpc = cw // LANES
    rs = DENSE_RS
    nsl = eb // rs
    assert d_model // rs == nsl

    @pl.when(j == 0)
    def _():
        acc_scr[...] = jnp.zeros_like(acc_scr)

    def slice_tiles(i):
        return [(k, l) for k in range(i * rs // N_KEYS, (i + 1) * rs // N_KEYS) for l in range(lpc)]

    def score_slice(c, i):
        a = lax.dot_general(u_ref[i * rs:(i + 1) * rs, :], h2_ref[c * cw:(c + 1) * cw, :], NT_DIMS,
                            preferred_element_type=F32)
        a_scr[c % 2, i * rs:(i + 1) * rs, :] = _gelu_tanh_sigmoid_form(a.astype(BF16))

    def weight_slice(c, i):
        for k, l in slice_tiles(i):
            rows = slice(k * N_KEYS, (k + 1) * N_KEYS)
            heads = slice(k * PEER_HEADS, (k + 1) * PEER_HEADS)
            lanes = slice(l * LANES, (l + 1) * LANES)
            lt = c * lpc + l
            e1_all = e1_ref[lt, heads, :]
            n1_all = n1_ref[lt, heads, :]
            gate = None
            for h in range(PEER_HEADS):
                e1 = jnp.broadcast_to(e1_all[h:h + 1, :], (N_KEYS, LANES)).astype(BF16)
                n1 = jnp.broadcast_to(n1_all[h:h + 1, :], (N_KEYS, LANES)).astype(BF16)
                term = jnp.minimum(e1 * e2_ref[h, lt], jnp.maximum(n1 - r2_ref[h, lt], jnp.zeros((), BF16)))
                gate = term if gate is None else gate + term
            g_scr[c % 2, rows, lanes] = gate * a_scr[c % 2, rows, lanes]

    def output_slice(c, i):
        acc_scr[c, i * rs:(i + 1) * rs, :] += jnp.dot(vT_ref[i * rs:(i + 1) * rs, :], g_scr[c % 2],
                                                      preferred_element_type=F32)

    for c in range(nchunk + 1):
        for i in range(nsl):
            if c < nchunk:
                score_slice(c, i)
            if c > 0:
                output_slice(c - 1, i)
            if c < nchunk and i > 0:
                weight_slice(c, i - 1)
        if c < nchunk:
            weight_slice(c, nsl - 1)

    @pl.when(j == pl.num_programs(2) - 1)
    def _():
        for c in range(nchunk):
            tok = slice(c * cw, (c + 1) * cw)
            x2 = x1_ref[tok, :] + gt_ref[...] * acc_scr[c].T
            if final_norm:
                x2 = x2 * lax.rsqrt(jnp.mean(x2 * x2, axis=-1, keepdims=True) + EPS) * fg_ref[...]
            o_ref[tok, :] = x2


def _dense(h2, u_bf, vT_bf, e1n, n1, e2, r2, x1, gt2, fg, final_norm, tt, eb):
    B, S, D = x1.shape
    E = u_bf.shape[0]
    nt = S // tt
    rb = eb // N_KEYS * PEER_HEADS
    cw = min(tt, DENSE_CW)
    assert eb == D and eb % DENSE_RS == 0
    gate_spec = pl.BlockSpec((tt // LANES, rb, LANES), lambda b, i, j: (b * nt + i, j, 0))
    tile_spec = pl.BlockSpec((PEER_HEADS, tt // LANES, N_KEYS, LANES), lambda b, i, j: (0, b * nt + i, 0, 0))
    return pl.pallas_call(
        functools.partial(_dense_kernel, eb, final_norm), name="dense",
        grid=(B, nt, E // eb),
        in_specs=[pl.BlockSpec((None, tt, D), lambda b, i, j: (b, i, 0)),
                  pl.BlockSpec((eb, D), lambda b, i, j: (j, 0)),
                  pl.BlockSpec((D, eb), lambda b, i, j: (0, j)),
                  gate_spec, gate_spec, tile_spec, tile_spec,
                  pl.BlockSpec((None, tt, D), lambda b, i, j: (b, i, 0)),
                  pl.BlockSpec((None, 1, D), lambda b, i, j: (b, 0, 0)),
                  pl.BlockSpec((1, D), lambda b, i, j: (0, 0))],
        out_specs=pl.BlockSpec((None, tt, D), lambda b, i, j: (b, i, 0)),
        out_shape=jax.ShapeDtypeStruct((B, S, D), F32),
        scratch_shapes=[pltpu.VMEM((2, eb, cw), BF16), pltpu.VMEM((2, eb, cw), BF16),
                        pltpu.VMEM((tt // cw, D, cw), F32)],
        compiler_params=_cparams(3),
    )(h2, u_bf, vT_bf, e1n, n1, e2, r2, x1, gt2, fg)


def _tile(n, pref):
    return pref if n % pref == 0 else n


def kernel(x, c, w_ada, b_ada, norm1_g, w_in, conv_w, conv_b, b_igate, b_fgate, rel_bias, mlstm_norm_g,
           w_branch_a, w_branch_b, w_out, norm2_g, w_peer_q, peer_sub_keys, peer_u, peer_v, final_g):
    B, S, D = x.shape
    depth = w_ada.shape[0]
    for l in range(depth):
        mod = _mod(c, w_ada[l], b_ada[l])
        sh1, sc1, gt1, sh2, sc2, gt2 = [m.reshape(B, 1, D) for m in jnp.split(mod, 6, axis=-1)]

        o = [0]
        for sz in (A_WIDTH, A_WIDTH, A_WIDTH, 2 * B_WIDTH, B_WIDTH, B_WIDTH, B_HEADS, B_HEADS, D, D):
            o.append(o[-1] + sz)
        wl = w_in[l]
        seg = lambda i: wl[:, o[i]:o[i + 1]]
        w_main = jnp.concatenate([seg(8), seg(9), seg(0), seg(1), seg(2), seg(3), seg(4), seg(5)], axis=1).astype(BF16)
        w_gate = jnp.concatenate([seg(6), seg(7), jnp.zeros((D, LANES - 2 * B_HEADS), F32)], axis=1).astype(BF16)
        w_gateT = w_gate[:, :SUBLANES].T
        gb = jnp.concatenate([b_igate[l], b_fgate[l]]).astype(F32)
        tm = _tile(S, 512)
        gbias = jnp.concatenate([gb, jnp.zeros((LANES - 2 * B_HEADS,), F32)]).reshape(1, LANES)
        gbiasT = jnp.broadcast_to(gb[:, None], (SUBLANES, tm))
        proj, gates, gatesT = _inproj(x, sh1, sc1, norm1_g[l].reshape(1, D), w_main, w_gate, w_gateT,
                                      gbias, gbiasT, tm, P_COLS)

        tq = _tile(S, 256)
        y_a = _attention(proj, rel_bias[l], tq, 2 if (S // tq) % 2 == 0 else 1)
        y_b = _mlstm(proj, gates, gatesT, conv_w[l], conv_b[l], mlstm_norm_g[l], _tile(S, 256), _tile(B, 2))
        x1, h2 = _merge(y_a, y_b, proj, x, gt1, w_branch_a[l].astype(BF16), w_branch_b[l].astype(BF16),
                        w_out[l].astype(BF16), norm2_g[l].reshape(1, D), sh2, sc2, _tile(S, 512))

        ws = _fold(w_peer_q[l], peer_sub_keys[l])
        e1n, n1, e2, r2 = _select(h2.reshape(B * S, D), ws, _tile(B * S, 512))
        x = _dense(h2, peer_u[l].astype(BF16), peer_v[l].T.astype(BF16), e1n, n1, e2, r2, x1, gt2,
                   final_g.reshape(1, D), l == depth - 1, _tile(S, 1024), 1024)
    return x
```
